```python
import math
import jax, jax.numpy as jnp
from jax import lax
import numpy as np

D_MODEL = 1024
BATCH = 2
SEQ = 8192
DEPTH = 2
DEC_BATCH = 128
DEC_SEQ = 4
PAST_LEN = 16384
PAGE_SIZE = 128

HEAD_DIM = 64
GROUP_HEADS = 4
W_GROUP = GROUP_HEADS * HEAD_DIM
N_GROUPS = 4
D_MIX = N_GROUPS * W_GROUP
CONV_W = 4
LRU_C = 8.0
LRU_BLOCKS = 4
LRU_FLOOR = 1e-12
CHUNK_B = 128
WINDOW = 128
N_Q_C = 4
N_KV_C = 2
ROPE_THETA = 10000.0
NEG_BIG = -1e30
HGRN_HEADS = 4
HGRN_DK = 64
HGRN_DV = 64
CHUNK_D = 64
PEER_HEADS = 8
PEER_KEYS = 128
PEER_N = PEER_KEYS * PEER_KEYS
PEER_DQ = 128
PEER_TOPK = 16
PEER_BLOCK = 128
EPS = 1e-6
IN_SIZES = (W_GROUP, W_GROUP, W_GROUP, W_GROUP, N_Q_C * HEAD_DIM, N_KV_C * HEAD_DIM,
            N_KV_C * HEAD_DIM, W_GROUP, W_GROUP, W_GROUP, W_GROUP)
D_IN = sum(IN_SIZES)
F32 = jnp.float32

kernel_name = 'hybrid_rglru_sgu_swa_hgrn2_peer'


def _rmsnorm(x, w):
    x32 = x.astype(F32)
    y = x32 * lax.rsqrt(jnp.mean(x32 * x32, axis=-1, keepdims=True) + EPS)
    return (y * w.astype(F32)).astype(x.dtype)


def _modulate(h, shift, scale):
    return h * (1.0 + scale[:, None, :]) + shift[:, None, :]


def _rope(x, pos):
    half = x.shape[-1] // 2
    freqs = ROPE_THETA ** (-jnp.arange(half, dtype=F32) / half)
    ang = pos.astype(F32)[:, None] * freqs[None, :]
    cos = jnp.cos(ang)[None, :, None, :]
    sin = jnp.sin(ang)[None, :, None, :]
    x32 = x.astype(F32)
    x1, x2 = x32[..., :half], x32[..., half:]
    return jnp.concatenate([x1 * cos - x2 * sin, x2 * cos + x1 * sin], axis=-1).astype(x.dtype)


def _lru_combine(e1, e2):
    a1, b1 = e1
    a2, b2 = e2
    return a1 * a2, a2 * b1 + b2


def _rglru(xa, ga, buf0, h0, conv_w, conv_b, wa, ba, wx, bx, lam):
    B, T, W = xa.shape
    xfull = jnp.concatenate([buf0.astype(xa.dtype), xa], axis=1)
    xc = conv_b + sum(conv_w[k] * xfull[:, k:k + T] for k in range(CONV_W))
    xh = xc.reshape(B, T, LRU_BLOCKS, W // LRU_BLOCKS)
    r = jax.nn.sigmoid((jnp.einsum('bthi,hij->bthj', xh, wa).reshape(B, T, W) + ba).astype(F32))
    ig = jax.nn.sigmoid((jnp.einsum('bthi,hij->bthj', xh, wx).reshape(B, T, W) + bx).astype(F32))
    log_a = -LRU_C * jax.nn.softplus(-lam.astype(F32)) * r
    a = jnp.exp(log_a)
    mult = jnp.sqrt(jnp.maximum(-jnp.expm1(2.0 * log_a), LRU_FLOOR))
    b = mult * (ig * xc.astype(F32))
    b = b.at[:, 0].add(a[:, 0] * h0.astype(F32))
    _, h = lax.associative_scan(_lru_combine, (a, b), axis=1)
    y = h * jax.nn.gelu(ga.astype(F32))
    return y.astype(xa.dtype), xfull[:, T:], h[:, -1]


def _chunk_mlp(u, v, gain, ws, bs):
    B, T, W = u.shape
    u = jax.nn.gelu(u)
    v = _rmsnorm(jax.nn.gelu(v), gain)
    pad = (-T) % CHUNK_B
    nc = (T + pad) // CHUNK_B
    vp = jnp.pad(v, ((0, 0), (0, pad), (0, 0))).reshape(B, nc, CHUNK_B, GROUP_HEADS, HEAD_DIM)
    causal = jnp.tril(jnp.ones((CHUNK_B, CHUNK_B), dtype=ws.dtype))
    mix = jnp.einsum('gts,bcsgd->bctgd', ws * causal, vp) + bs.T[:, :, None]
    mix = mix.reshape(B, nc * CHUNK_B, W)[:, :T]
    return u * mix, v


def _sink_softmax(s, sink):
    m = jnp.maximum(jnp.max(s, axis=-1, keepdims=True), sink)
    e = jnp.exp(s - m)
    return e / (jnp.sum(e, axis=-1, keepdims=True) + jnp.exp(sink - m))


def _swa_prompt(q, k, v, sinks):
    B, T, HQ, D = q.shape
    G = HQ // N_KV_C
    nb = T // WINDOW
    qb = q.reshape(B, nb, WINDOW, N_KV_C, G, D)
    kb = k.reshape(B, nb, WINDOW, N_KV_C, D)
    vb = v.reshape(B, nb, WINDOW, N_KV_C, D)
    kk = jnp.concatenate([jnp.concatenate([jnp.zeros_like(kb[:, :1]), kb[:, :-1]], axis=1), kb], axis=2)
    vv = jnp.concatenate([jnp.concatenate([jnp.zeros_like(vb[:, :1]), vb[:, :-1]], axis=1), vb], axis=2)
    s = jnp.einsum('bnqhgd,bnkhd->bnhgqk', qb, kk).astype(F32) * (D ** -0.5)
    blk = jnp.arange(nb)[:, None, None]
    qi = jnp.arange(WINDOW)[None, :, None] + WINDOW
    ki = jnp.arange(2 * WINDOW)[None, None, :]
    diff = qi - ki
    mask = (diff >= 0) & (diff < WINDOW) & (blk * WINDOW + ki - WINDOW >= 0)
    s = jnp.where(mask[None, :, None, None], s, NEG_BIG)
    p = _sink_softmax(s, sinks.astype(F32).reshape(N_KV_C, G, 1, 1))
    o = jnp.einsum('bnhgqk,bnkhd->bnqhgd', p.astype(v.dtype), vv).reshape(B, T, HQ * D)
    L = min(WINDOW, T)
    return o, k[:, -L:], v[:, -L:]


def _swa_sample(q, k, v, k_buf, v_buf, sinks):
    B, T, HQ, D = q.shape
    G = HQ // N_KV_C
    L = k_buf.shape[1]
    kk = jnp.concatenate([k_buf.astype(k.dtype), k], axis=1)
    vv = jnp.concatenate([v_buf.astype(v.dtype), v], axis=1)
    qg = q.reshape(B, T, N_KV_C, G, D)
    s = jnp.einsum('bqhgd,bkhd->bhgqk', qg, kk).astype(F32) * (D ** -0.5)
    diff = (L + jnp.arange(T))[:, None] - jnp.arange(L + T)[None, :]
    mask = (diff >= 0) & (diff < WINDOW)
    s = jnp.where(mask, s, NEG_BIG)
    p = _sink_softmax(s, sinks.astype(F32).reshape(N_KV_C, G, 1, 1))
    o = jnp.einsum('bhgqk,bkhd->bqhgd', p.astype(v.dtype), vv).reshape(B, T, HQ * D)
    return o, kk[:, -L:], vv[:, -L:]


def _gla_chunked(q, k, v, logf, S0):
    B, T, H, DK = q.shape
    DV = v.shape[-1]
    C = math.gcd(T, CHUNK_D)
    n = T // C

    def to_chunks(t):
        return jnp.moveaxis(t.reshape(B, n, C, H, t.shape[-1]), 1, 0)

    causal = jnp.tril(jnp.ones((C, C), dtype=bool))[None, :, :, None, None]

    def step(S, inp):
        qc, kc, vc, gc = inp
        bc = jnp.cumsum(gc, axis=1)
        o_inter = jnp.einsum('bthk,bhkv->bthv', qc * jnp.exp(bc), S)
        rel = jnp.minimum(bc[:, :, None] - bc[:, None, :], 0.0)
        decay = jnp.where(causal, jnp.exp(rel), 0.0)
        att = jnp.einsum('bthk,btshk,bshk->bhts', qc, decay, kc)
        o_intra = jnp.einsum('bhts,bshv->bthv', att, vc)
        bl = bc[:, -1]
        S_new = jnp.exp(bl)[..., None] * S + jnp.einsum('bshk,bshv->bhkv', kc * jnp.exp(bl[:, None] - bc), vc)
        return S_new, o_inter + o_intra

    S, o = lax.scan(step, S0, (to_chunks(q), to_chunks(k), to_chunks(v), to_chunks(logf)))
    return jnp.moveaxis(o, 0, 1).reshape(B, T, H, DV), S


def _hgrn2(qd, fd, idd, gd, S0, lb, gnorm):
    B, T, _ = qd.shape
    shp = (B, T, HGRN_HEADS, HGRN_DK)
    z = fd.astype(F32)
    lb = lb.astype(F32)
    sz = jax.nn.sigmoid(z)
    f = lb + (1.0 - lb) * sz
    logf = jnp.log(jnp.maximum(f, 1e-30))
    k = (1.0 - lb) * jax.nn.sigmoid(-z)
    q = jax.nn.silu(qd.astype(F32))
    v = idd.astype(F32).reshape(B, T, HGRN_HEADS, HGRN_DV)
    o, S = _gla_chunked(q.reshape(shp), k.reshape(shp), v, logf.reshape(shp), S0.astype(F32))
    o = _rmsnorm(o, gnorm) * jax.nn.silu(gd.astype(F32).reshape(B, T, HGRN_HEADS, HGRN_DV))
    return o.reshape(B, T, W_GROUP), S


def _peer(h, wq, keys, U, V):
    B, T, D = h.shape
    n = B * T
    pad = (-n) % PEER_BLOCK
    xb = jnp.pad(h.reshape(n, D), ((0, pad), (0, 0))).reshape(-1, PEER_BLOCK, D)

    def block(xblk):
        q = (xblk @ wq).reshape(PEER_BLOCK, PEER_HEADS, 2, PEER_DQ // 2)
        s = jnp.einsum('nhpd,hpkd->nhpk', q, keys).astype(F32)
        s1, i1 = lax.top_k(s[:, :, 0], PEER_TOPK)
        s2, i2 = lax.top_k(s[:, :, 1], PEER_TOPK)
        cand = (s1[..., :, None] + s2[..., None, :]).reshape(PEER_BLOCK, PEER_HEADS, PEER_TOPK * PEER_TOPK)
        sc, ci = lax.top_k(cand, PEER_TOPK)
        e1 = jnp.take_along_axis(i1, ci // PEER_TOPK, axis=-1)
        e2 = jnp.take_along_axis(i2, ci % PEER_TOPK, axis=-1)
        idx = e1 * PEER_KEYS + e2
        g = jax.nn.softmax(sc, axis=-1)
        act = jax.nn.gelu(jnp.einsum('nd,nhkd->nhk', xblk, jnp.take(U, idx, axis=0)).astype(F32))
        return jnp.einsum('nhk,nhkd->nd', (g * act).astype(V.dtype), jnp.take(V, idx, axis=0))

    out = lax.map(block, xb).reshape(-1, D)[:n]
    return out.reshape(B, T, D).astype(h.dtype)


def _layer(x, c, pos, lru_h0, lru_buf0, k_buf, v_buf, S0, p):
    B, T, _ = x.shape
    mod = jax.nn.silu(c) @ p['w_ada'] + p['b_ada']
    sh1, sc1, g1, sh2, sc2, g2 = jnp.split(mod, 6, axis=-1)
    h = _modulate(_rmsnorm(x, p['norm_mix']), sh1, sc1)
    splits = [int(s) for s in np.cumsum(IN_SIZES)[:-1]]
    xa, ga, u, vb, qc, kc, vc, qd, fd, idd, gd = jnp.split(h @ p['w_in'], splits, axis=-1)
    yA, lru_buf, lru_h = _rglru(xa, ga, lru_buf0, lru_h0, p['conv_w'], p['conv_b'], p['lru_wa'],
                                p['lru_ba'], p['lru_wx'], p['lru_bx'], p['lru_lambda'])
    yB, v_rows = _chunk_mlp(u, vb, p['sgu_norm'], p['sgu_ws'], p['sgu_b'])
    q = _rope(_rmsnorm(qc.reshape(B, T, N_Q_C, HEAD_DIM), p['q_norm']), pos)
    k = _rope(_rmsnorm(kc.reshape(B, T, N_KV_C, HEAD_DIM), p['k_norm']), pos)
    v = vc.reshape(B, T, N_KV_C, HEAD_DIM)
    if k_buf is None:
        yC, k_new, v_new = _swa_prompt(q, k, v, p['sinks'])
    else:
        yC, k_new, v_new = _swa_sample(q, k, v, k_buf, v_buf, p['sinks'])
    yD, S = _hgrn2(qd, fd, idd, gd, S0, p['lower_bound'], p['hgrn_gnorm'])
    y = jnp.concatenate([t.astype(x.dtype) for t in (yA, yB, yC, yD)], axis=-1).reshape(B, T, N_GROUPS, W_GROUP)
    y = _rmsnorm(y, p['out_norm'].reshape(N_GROUPS, W_GROUP)).reshape(B, T, D_MIX)
    x = x + g1[:, None] * (y @ p['w_out'])
    h2 = _modulate(_rmsnorm(x, p['norm_ffn']), sh2, sc2)
    x = x + g2[:, None] * _peer(h2, p['peer_wq'], p['peer_keys'], p['peer_u'], p['peer_v'])
    return x, (lru_h, lru_buf, k_new, v_new, S, v_rows)


def _stack(per_layer, j):
    return jnp.stack([st[j] for st in per_layer], axis=1)


def setup_inputs(seed: int = 0) -> dict:
    key = jax.random.key(seed)
    keys = list(jax.random.split(key, 40))

    def nrm(i, shape, scale):
        return scale * jax.random.normal(keys[i], shape, F32)

    def gain(i, shape):
        return 1.0 + 0.05 * jax.random.normal(keys[i], shape, F32)

    win_buf = min(WINDOW, PAST_LEN)
    bw = W_GROUP // LRU_BLOCKS
    s_lam = jax.random.uniform(keys[20], (DEPTH, W_GROUP), F32, minval=0.9, maxval=0.999) ** (1.0 / LRU_C)
    return {
        'x_prompt': nrm(0, (BATCH, SEQ, D_MODEL), 1.0),
        'x_sample': nrm(1, (DEC_BATCH, DEC_SEQ, D_MODEL), 1.0),
        'state_lru_h': nrm(2, (DEC_BATCH, DEPTH, W_GROUP), 0.5),
        'state_lru_conv': nrm(3, (DEC_BATCH, DEPTH, CONV_W - 1, W_GROUP), 1.0),
        'state_swa_k': nrm(4, (DEC_BATCH, DEPTH, win_buf, N_KV_C, HEAD_DIM), 1.0),
        'state_swa_v': nrm(5, (DEC_BATCH, DEPTH, win_buf, N_KV_C, HEAD_DIM), 1.0),
        'state_hgrn_S': nrm(6, (DEC_BATCH, DEPTH, HGRN_HEADS, HGRN_DK, HGRN_DV), 0.5),
        'c_prompt': nrm(7, (BATCH, D_MODEL), 1.0),
        'c_sample': nrm(8, (DEC_BATCH, D_MODEL), 1.0),
        'w_ada': nrm(9, (DEPTH, D_MODEL, 6 * D_MODEL), 0.5 * D_MODEL ** -0.5),
        'b_ada': nrm(10, (DEPTH, 6 * D_MODEL), 0.02),
        'norm_mix': gain(11, (DEPTH, D_MODEL)),
        'w_in': nrm(12, (DEPTH, D_MODEL, D_IN), D_MODEL ** -0.5),
        'conv_w': nrm(13, (DEPTH, CONV_W, W_GROUP), CONV_W ** -0.5),
        'conv_b': nrm(14, (DEPTH, W_GROUP), 0.02),
        'lru_wa': nrm(15, (DEPTH, LRU_BLOCKS, bw, bw), bw ** -0.5),
        'lru_ba': nrm(16, (DEPTH, W_GROUP), 0.02),
        'lru_wx': nrm(17, (DEPTH, LRU_BLOCKS, bw, bw), bw ** -0.5),
        'lru_bx': nrm(18, (DEPTH, W_GROUP), 0.02),
        'lru_lambda': jnp.log(s_lam) - jnp.log1p(-s_lam),
        'sgu_norm': gain(21, (DEPTH, W_GROUP)),
        'sgu_ws': nrm(22, (DEPTH, GROUP_HEADS, CHUNK_B, CHUNK_B), CHUNK_B ** -0.5),
        'sgu_b': 1.0 + nrm(23, (DEPTH, GROUP_HEADS, CHUNK_B), 0.02),
        'q_norm': gain(24, (DEPTH, HEAD_DIM)),
        'k_norm': gain(25, (DEPTH, HEAD_DIM)),
        'sinks': nrm(26, (DEPTH, N_Q_C), 0.5),
        'hgrn_lb': nrm(27, (DEPTH, W_GROUP), 0.5),
        'hgrn_gnorm': gain(28, (DEPTH, HGRN_DV)),
        'out_norm': gain(29, (DEPTH, D_MIX)),
        'w_out': nrm(30, (DEPTH, D_MIX, D_MODEL), D_MIX ** -0.5),
        'norm_ffn': gain(31, (DEPTH, D_MODEL)),
        'peer_wq': nrm(32, (DEPTH, D_MODEL, PEER_HEADS * PEER_DQ), D_MODEL ** -0.5),
        'peer_keys': nrm(33, (DEPTH, PEER_HEADS, 2, PEER_KEYS, PEER_DQ // 2), (PEER_DQ // 2) ** -0.5),
        'peer_u': nrm(34, (DEPTH, PEER_N, D_MODEL), D_MODEL ** -0.5),
        'peer_v': nrm(35, (DEPTH, PEER_N, D_MODEL), 1.0),
    }


def reference(x_prompt, x_sample, state_lru_h, state_lru_conv, state_swa_k, state_swa_v, state_hgrn_S,
              c_prompt, c_sample, w_ada, b_ada, norm_mix, w_in, conv_w, conv_b, lru_wa, lru_ba, lru_wx,
              lru_bx, lru_lambda, sgu_norm, sgu_ws, sgu_b, q_norm, k_norm, sinks, hgrn_lb, hgrn_gnorm,
              out_norm, w_out, norm_ffn, peer_wq, peer_keys, peer_u, peer_v):
    lbp = jax.nn.softmax(hgrn_lb.astype(F32), axis=0)
    lower_bound = jnp.cumsum(lbp, axis=0) - lbp[0]
    pos_p = jnp.arange(x_prompt.shape[1], dtype=jnp.int32)
    pos_s = PAST_LEN + jnp.arange(x_sample.shape[1], dtype=jnp.int32)
    Bp = x_prompt.shape[0]
    zeros_h = jnp.zeros((Bp, W_GROUP), F32)
    zeros_buf = jnp.zeros((Bp, CONV_W - 1, W_GROUP), x_prompt.dtype)
    zeros_S = jnp.zeros((Bp, HGRN_HEADS, HGRN_DK, HGRN_DV), F32)
    xp, xs = x_prompt, x_sample
    st_prompt, st_sample = [], []
    for i in range(DEPTH):
        p = {'w_ada': w_ada[i], 'b_ada': b_ada[i], 'norm_mix': norm_mix[i], 'w_in': w_in[i],
             'conv_w': conv_w[i], 'conv_b': conv_b[i], 'lru_wa': lru_wa[i], 'lru_ba': lru_ba[i],
             'lru_wx': lru_wx[i], 'lru_bx': lru_bx[i], 'lru_lambda': lru_lambda[i],
             'sgu_norm': sgu_norm[i], 'sgu_ws': sgu_ws[i], 'sgu_b': sgu_b[i],
             'q_norm': q_norm[i], 'k_norm': k_norm[i], 'sinks': sinks[i],
             'lower_bound': lower_bound[i], 'hgrn_gnorm': hgrn_gnorm[i], 'out_norm': out_norm[i],
             'w_out': w_out[i], 'norm_ffn': norm_ffn[i], 'peer_wq': peer_wq[i],
             'peer_keys': peer_keys[i], 'peer_u': peer_u[i], 'peer_v': peer_v[i]}
        xp, sp = _layer(xp, c_prompt, pos_p, zeros_h, zeros_buf, None, None, zeros_S, p)
        xs, ss = _layer(xs, c_sample, pos_s, state_lru_h[:, i], state_lru_conv[:, i],
                        state_swa_k[:, i], state_swa_v[:, i], state_hgrn_S[:, i], p)
        st_prompt.append(sp)
        st_sample.append(ss)
    return (xp, xs,
            _stack(st_prompt, 0), _stack(st_prompt, 1), _stack(st_prompt, 2), _stack(st_prompt, 3),
            _stack(st_prompt, 4),
            _stack(st_sample, 0), _stack(st_sample, 1), _stack(st_sample, 2), _stack(st_sample, 3),
            _stack(st_sample, 4), _stack(st_sample, 5))
```

```python
import functools

import numpy as np
import jax
import jax.numpy as jnp
from jax import lax
from jax.experimental import pallas as pl
from jax.experimental.pallas import tpu as pltpu

F32 = jnp.float32
BF = jnp.bfloat16

D_MODEL = 1024
DEPTH = 2
PAST_LEN = 16384
HEAD_DIM = 64
W_GROUP = 256
N_GROUPS = 4
CONV_W = 4
LRU_C = 8.0
LRU_BLOCKS = 4
LRU_FLOOR = 1e-12
CHUNK_B = 128
WINDOW = 128
N_Q_C = 4
N_KV_C = 2
ROPE_THETA = 10000.0
NEG_BIG = -1e30
HGRN_HEADS = 4
PEER_HEADS = 8
PEER_KEYS = 128
PEER_DQ = 128
PEER_TOPK = 16
EPS = 1e-6
D_IN = 2560

LANES = 128
SUBLANES = 8
VMEM_LIMIT = 56 * 1024 * 1024

HG_SUB = 16
HG_SUPER = 128


def _cparams(sem):
    return pltpu.CompilerParams(dimension_semantics=sem, vmem_limit_bytes=VMEM_LIMIT)


_NN = (((1,), (0,)), ((), ()))
_NT = (((1,), (1,)), ((), ()))


def _dg(a, b, dims):
    return lax.dot_general(a, b, dims, preferred_element_type=F32)


def _hi_lo(a):
    hi = a.astype(BF)
    lo = (a - hi.astype(F32)).astype(BF)
    return hi, lo


def _mm3(a, b, dims=_NN):
    ah, al = _hi_lo(a)
    bh, bl = _hi_lo(b)
    return (_dg(ah, bl, dims) + _dg(al, bh, dims)) + _dg(ah, bh, dims)


def _split3(a):
    a1 = a.astype(BF)
    r1 = a - a1.astype(F32)
    a2 = r1.astype(BF)
    a3 = (r1 - a2.astype(F32)).astype(BF)
    return a1, a2, a3


def _mm_x01(a, b01, dims=_NN):
    a1, a2, a3 = _split3(a)
    return (_dg(a3, b01, dims) + _dg(a2, b01, dims)) + _dg(a1, b01, dims)


def _mm_01x(a01, b, dims=_NN):
    b1, b2, b3 = _split3(b)
    return (_dg(a01, b3, dims) + _dg(a01, b2, dims)) + _dg(a01, b1, dims)


def _sigmoid(x):
    return jax.nn.sigmoid(x)


def _silu(x):
    return x * jax.nn.sigmoid(x)


def _gelu(x):
    return x * (0.5 * (1.0 + jnp.tanh(0.7978845608028654 * (x + 0.044715 * (x * x * x)))))


def _rms(x, w):
    return x * lax.rsqrt(jnp.mean(x * x, axis=-1, keepdims=True) + EPS) * w


def _seg_ones(width, seg):
    r = lax.broadcasted_iota(jnp.int32, (width, width), 0) // seg
    c = lax.broadcasted_iota(jnp.int32, (width, width), 1) // seg
    return jnp.where(r == c, 1.0, 0.0).astype(BF)


def _head_rms(x, gain):
    ms = _mm_x01(x * x, _seg_ones(x.shape[-1], HEAD_DIM)) * (1.0 / HEAD_DIM)
    return x * lax.rsqrt(ms + EPS) * gain


def _rope(x, cos_f, sin_s):
    w = x.shape[-1]
    lane = lax.broadcasted_iota(jnp.int32, x.shape, x.ndim - 1) % HEAD_DIM
    rot = jnp.where(lane < HEAD_DIM // 2,
                    pltpu.roll(x, w - HEAD_DIM // 2, x.ndim - 1),
                    pltpu.roll(x, HEAD_DIM // 2, x.ndim - 1))
    return x * cos_f + rot * sin_s


def _lane_head_mask(width, j):
    lane = lax.broadcasted_iota(jnp.int32, (1, width), 1) // HEAD_DIM
    return jnp.where(lane == j, 1.0, 0.0)


def _ada_body(c_ref, w_ref, b_ref, o_ref):
    c = _silu(c_ref[...])
    o_ref[...] = _mm3(c, w_ref[...]) + b_ref[...]


def _ada(c_all, w_ada, b_ada):
    n = c_all.shape[0]
    cb = 1536
    return pl.pallas_call(
        _ada_body,
        grid=(DEPTH, 6 * D_MODEL // cb),
        in_specs=[
            pl.BlockSpec((n, D_MODEL), lambda l, j: (0, 0)),
            pl.BlockSpec((None, D_MODEL, cb), lambda l, j: (l, 0, j)),
            pl.BlockSpec((None, 1, cb), lambda l, j: (l, 0, j)),
        ],
        out_specs=pl.BlockSpec((None, n, cb), lambda l, j: (l, 0, j)),
        out_shape=jax.ShapeDtypeStruct((DEPTH, n, 6 * D_MODEL), F32),
        compiler_params=_cparams(("arbitrary", "arbitrary")),
        name="ada",
    )(c_all, w_ada, b_ada.reshape(DEPTH, 1, 6 * D_MODEL))


class _Rows:
    def __init__(self, nb, nt, tn, per_row_mod):
        self.nb, self.nt, self.tn, self.per_row_mod = nb, nt, tn, per_row_mod

    def spec(self, width, col=0):
        nt = self.nt
        return pl.BlockSpec((self.tn, width), lambda b, t: (b * nt + t, col))

    def mod_spec(self, j):
        if self.per_row_mod:
            nt = self.nt
            return pl.BlockSpec((self.tn, D_MODEL), lambda b, t: (b * nt + t, j))
        return pl.BlockSpec((None, 1, D_MODEL), lambda b, t: (b, 0, j))

    @property
    def grid(self):
        return (self.nb, self.nt)

    @property
    def rows(self):
        return self.nb * self.nt * self.tn


def _const_spec(shape):
    nd = len(shape)
    return pl.BlockSpec(shape, lambda b, t: (0,) * nd)


def _in_body(x_ref, nw_ref, sh_ref, sc_ref, w_ref, z_ref):
    h = _rms(x_ref[...], nw_ref[...])
    h = h * (1.0 + sc_ref[...]) + sh_ref[...]
    z_ref[...] = jnp.dot(h.astype(BF), w_ref[...], preferred_element_type=F32)


def _in_proj(rows, x, mod, nw, w_bf):
    return pl.pallas_call(
        _in_body,
        grid=rows.grid,
        in_specs=[rows.spec(D_MODEL), _const_spec((1, D_MODEL)),
                  rows.mod_spec(0), rows.mod_spec(1), _const_spec((D_MODEL, D_IN))],
        out_specs=rows.spec(D_IN),
        out_shape=jax.ShapeDtypeStruct((rows.rows, D_IN), F32),
        compiler_params=_cparams(("arbitrary", "arbitrary")),
        name="in_proj",
    )(x, nw, mod, mod, w_bf)


def _out_body(x_ref, ya_ref, yb_ref, yc_ref, yd_ref, on_ref, w_ref, g1_ref, nf_ref,
              sh_ref, sc_ref, x1_ref, h2_ref):
    ys = []
    for g, r in enumerate((ya_ref, yb_ref, yc_ref, yd_ref)):
        ys.append(_rms(r[...], on_ref[:, g * W_GROUP:(g + 1) * W_GROUP]).astype(BF))
    o = jnp.dot(jnp.concatenate(ys, axis=-1), w_ref[...], preferred_element_type=F32)
    x1 = x_ref[...] + g1_ref[...] * o
    x1_ref[...] = x1
    h2 = _rms(x1, nf_ref[...])
    h2_ref[...] = h2 * (1.0 + sc_ref[...]) + sh_ref[...]


def _out_proj(rows, x, ys, mod, on, w_bf, nf):
    sds = jax.ShapeDtypeStruct((rows.rows, D_MODEL), F32)
    return pl.pallas_call(
        _out_body,
        grid=rows.grid,
        in_specs=[rows.spec(D_MODEL)] + [rows.spec(W_GROUP)] * 4
        + [_const_spec((1, D_MODEL)), _const_spec((D_MODEL, D_MODEL)), rows.mod_spec(2),
           _const_spec((1, D_MODEL)), rows.mod_spec(3), rows.mod_spec(4)],
        out_specs=[rows.spec(D_MODEL), rows.spec(D_MODEL)],
        out_shape=[sds, sds],
        compiler_params=_cparams(("arbitrary", "arbitrary")),
        name="out_proj",
    )(x, *ys, on, w_bf, mod, nf, mod, mod)


def _lru_gates(xc, wa, wx, ba, bx, lam):
    r = _sigmoid(_mm3(xc, wa) + ba)
    ig = _sigmoid(_mm3(xc, wx) + bx)
    nl = -lam
    sp = jnp.maximum(nl, 0.0) + jnp.log1p(jnp.exp(-jnp.abs(nl)))
    log_a = (-LRU_C) * sp * r
    a = jnp.exp(log_a)
    x2 = 2.0 * log_a
    em1 = jnp.tanh(0.5 * x2) * (jnp.exp(x2) + 1.0)
    mult = jnp.sqrt(jnp.maximum(-em1, LRU_FLOOR))
    return a, mult * (ig * xc)


def _lru_prompt_body(z_ref, cw_ref, cb_ref, wa_ref, wx_ref, ba_ref, bx_ref, lam_ref,
                     y_ref, hfin_ref, tail_ref, xbuf, hcar, *, tb):
    t = pl.program_id(1)

    @pl.when(t == 0)
    def _():
        xbuf[0:SUBLANES, :] = jnp.zeros((SUBLANES, W_GROUP), F32)
        hcar[...] = jnp.zeros_like(hcar)

    xa = z_ref[:, 0:W_GROUP]
    ga = z_ref[:, W_GROUP:2 * W_GROUP]
    xbuf[SUBLANES:SUBLANES + tb, :] = xa
    xc = cb_ref[...] + cw_ref[3:4, :] * xa
    for k in range(CONV_W - 1):
        xc = xc + cw_ref[k:k + 1, :] * xbuf[pl.ds(SUBLANES - (CONV_W - 1) + k, tb), :]
    xbuf[0:SUBLANES, :] = xa[tb - SUBLANES:tb, :]
    tail_ref[...] = xa[tb - SUBLANES:tb, :]

    a, b = _lru_gates(xc, wa_ref[...], wx_ref[...], ba_ref[...], bx_ref[...], lam_ref[...])
    row = lax.broadcasted_iota(jnp.int32, (tb, W_GROUP), 0)
    d = 1
    while d < tb:
        a_s = pltpu.roll(a, d, 0)
        b_s = pltpu.roll(b, d, 0)
        m = row >= d
        b = jnp.where(m, a * b_s + b, b)
        a = jnp.where(m, a * a_s, a)
        d *= 2
    h = b + a * hcar[0:1, :]
    hl = h[tb - 1:tb, :]
    hcar[...] = jnp.broadcast_to(hl, hcar.shape)
    hfin_ref[...] = hl
    y_ref[...] = h * _gelu(ga)


def _lru_prompt(z, nb, t_len, p, tb=512):
    nt = t_len // tb
    w2 = (1, W_GROUP)
    return pl.pallas_call(
        functools.partial(_lru_prompt_body, tb=tb),
        grid=(nb, nt),
        in_specs=[pl.BlockSpec((tb, 2 * W_GROUP), lambda b, t: (b * nt + t, 0)),
                  _const_spec((CONV_W, W_GROUP)), _const_spec(w2),
                  _const_spec((W_GROUP, W_GROUP)), _const_spec((W_GROUP, W_GROUP)),
                  _const_spec(w2), _const_spec(w2), _const_spec(w2)],
        out_specs=[pl.BlockSpec((tb, W_GROUP), lambda b, t: (b * nt + t, 0)),
                   pl.BlockSpec((None, 1, W_GROUP), lambda b, t: (b, 0, 0)),
                   pl.BlockSpec((None, SUBLANES, W_GROUP), lambda b, t: (b, 0, 0))],
        out_shape=[jax.ShapeDtypeStruct((nb * t_len, W_GROUP), F32),
                   jax.ShapeDtypeStruct((nb, 1, W_GROUP), F32),
                   jax.ShapeDtypeStruct((nb, SUBLANES, W_GROUP), F32)],
        scratch_shapes=[pltpu.VMEM((tb + SUBLANES, W_GROUP), F32),
                        pltpu.VMEM((SUBLANES, W_GROUP), F32)],
        compiler_params=_cparams(("arbitrary", "arbitrary")),
        name="lru_prompt",
    )(z, p["conv_w"], p["conv_b"], p["wa_bd"], p["wx_bd"], p["lru_ba"], p["lru_bx"], p["lru_lambda"])


def _sgu_prompt_body(z_ref, gain_ref, ws_ref, bias_ref, y_ref, *, tb):
    u = _gelu(z_ref[:, 0:W_GROUP])
    v = _rms(_gelu(z_ref[:, W_GROUP:2 * W_GROUP]), gain_ref[...])
    masks = [_lane_head_mask(W_GROUP, g) for g in range(W_GROUP // HEAD_DIM)]
    ws = ws_ref[...]
    bias = bias_ref[...]
    for j in range(tb // CHUNK_B):
        vj = v[j * CHUNK_B:(j + 1) * CHUNK_B, :]
        rhs = jnp.concatenate([vj * m for m in masks], axis=0)
        mix = _mm3(ws, rhs) + bias
        y_ref[j * CHUNK_B:(j + 1) * CHUNK_B, :] = u[j * CHUNK_B:(j + 1) * CHUNK_B, :] * mix


def _sgu_prompt(z, nb, t_len, p, tb=512):
    nt = t_len // tb
    nh = W_GROUP // HEAD_DIM
    return pl.pallas_call(
        functools.partial(_sgu_prompt_body, tb=tb),
        grid=(nb, nt),
        in_specs=[pl.BlockSpec((tb, 2 * W_GROUP), lambda b, t: (b * nt + t, 1)),
                  _const_spec((1, W_GROUP)), _const_spec((CHUNK_B, nh * CHUNK_B)),
                  _const_spec((CHUNK_B, W_GROUP))],
        out_specs=pl.BlockSpec((tb, W_GROUP), lambda b, t: (b * nt + t, 0)),
        out_shape=jax.ShapeDtypeStruct((nb * t_len, W_GROUP), F32),
        compiler_params=_cparams(("arbitrary", "arbitrary")),
        name="sgu_prompt",
    )(z, p["sgu_norm"], p["sgu_ws_cat"], p["sgu_bias"])


def _ab_sample_body(z_ref, h0_ref, buf_ref, cw_ref, cb_ref, wa_ref, wx_ref, ba_ref, bx_ref,
                    lam_ref, gain_ref, mw_ref, mb_ref,
                    ya_ref, hfin_ref, nbuf_ref, yb_ref, vrow_ref, *, nb, nt):
    xa = z_ref[:, 0:W_GROUP]
    ga = z_ref[:, W_GROUP:2 * W_GROUP]
    slabs = [buf_ref[k] for k in range(CONV_W - 1)] + [xa[t * nb:(t + 1) * nb, :] for t in range(nt)]
    xcs = []
    for t in range(nt):
        xc = cb_ref[...]
        for k in range(CONV_W):
            xc = xc + cw_ref[k:k + 1, :] * slabs[t + k]
        xcs.append(xc)
    for k in range(CONV_W - 1):
        nbuf_ref[k] = slabs[nt + k]
    xc = jnp.concatenate(xcs, axis=0)
    a, b = _lru_gates(xc, wa_ref[...], wx_ref[...], ba_ref[...], bx_ref[...], lam_ref[...])
    h = h0_ref[...]
    hs = []
    for t in range(nt):
        h = a[t * nb:(t + 1) * nb, :] * h + b[t * nb:(t + 1) * nb, :]
        hs.append(h)
    hfin_ref[...] = h
    ya_ref[...] = jnp.concatenate(hs, axis=0) * _gelu(ga)

    u = _gelu(z_ref[:, 2 * W_GROUP:3 * W_GROUP])
    v = _rms(_gelu(z_ref[:, 3 * W_GROUP:4 * W_GROUP]), gain_ref[...])
    vrow_ref[...] = v
    for t in range(nt):
        mix = mb_ref[t]
        for s in range(t + 1):
            mix = mix + mw_ref[t, s] * v[s * nb:(s + 1) * nb, :]
        yb_ref[t * nb:(t + 1) * nb, :] = u[t * nb:(t + 1) * nb, :] * mix


def _ab_sample(z, h0, buf, p, nb, nt):
    n = nb * nt
    full = lambda shape: pl.BlockSpec(shape, lambda i: (0,) * len(shape))
    w2 = (1, W_GROUP)
    sd = lambda *s: jax.ShapeDtypeStruct(s, F32)
    return pl.pallas_call(
        functools.partial(_ab_sample_body, nb=nb, nt=nt),
        grid=(1,),
        in_specs=[full((n, 4 * W_GROUP)), full((nb, W_GROUP)), full((CONV_W - 1, nb, W_GROUP)),
                  full((CONV_W, W_GROUP)), full(w2), full((W_GROUP, W_GROUP)),
                  full((W_GROUP, W_GROUP)), full(w2), full(w2), full(w2), full(w2),
                  full((nt, nt, 1, W_GROUP)), full((nt, 1, W_GROUP))],
        out_specs=[full((n, W_GROUP)), full((nb, W_GROUP)), full((CONV_W - 1, nb, W_GROUP)),
                   full((n, W_GROUP)), full((n, W_GROUP))],
        out_shape=[sd(n, W_GROUP), sd(nb, W_GROUP), sd(CONV_W - 1, nb, W_GROUP),
                   sd(n, W_GROUP), sd(n, W_GROUP)],
        compiler_params=_cparams(("arbitrary",)),
        name="ab_sample",
    )(z, h0, buf, p["conv_w"], p["conv_b"], p["wa_bd"], p["wx_bd"], p["lru_ba"], p["lru_bx"],
      p["lru_lambda"], p["sgu_norm"], p["sgu_mw_s"], p["sgu_mb_s"])


def _swa_prompt_body(sink_ref, z_ref, cos_ref, sin_ref, qg_ref, kg_ref,
                     y_ref, klast_ref, vlast_ref, kprev, vprev, *, tb):
    t = pl.program_id(1)

    @pl.when(t == 0)
    def _():
        kprev[...] = jnp.zeros_like(kprev)
        vprev[...] = jnp.zeros_like(vprev)

    cos = cos_ref[...]
    sin = sin_ref[...]
    q = _head_rms(z_ref[:, 0:2 * LANES], qg_ref[...])
    q = _rope(q, jnp.concatenate([cos, cos], axis=-1), jnp.concatenate([sin, sin], axis=-1))
    k = _rope(_head_rms(z_ref[:, 2 * LANES:3 * LANES], kg_ref[...]), cos, sin)
    v = z_ref[:, 3 * LANES:4 * LANES]
    masks = [_lane_head_mask(LANES, j) for j in range(N_KV_C)]
    r = lax.broadcasted_iota(jnp.int32, (WINDOW, 2 * WINDOW), 0)
    c = lax.broadcasted_iota(jnp.int32, (WINDOW, 2 * WINDOW), 1)
    diff = r + WINDOW - c
    band = (diff >= 0) & (diff < WINDOW)
    nsub = tb // WINDOW
    for i in range(nsub):
        sl = slice(i * WINDOW, (i + 1) * WINDOW)
        ki, vi = k[sl, :], v[sl, :]
        kk = jnp.concatenate([kprev[...], ki], axis=0)
        vv = jnp.concatenate([vprev[...], vi], axis=0)
        first = (t * nsub + i) == 0
        ok = band & ((c >= WINDOW) | jnp.logical_not(first))
        outs = []
        for g in range(N_Q_C // N_KV_C):
            qg = q[sl, g * LANES:(g + 1) * LANES]
            og = jnp.zeros((WINDOW, LANES), F32)
            for j in range(N_KV_C):
                s = _mm3(qg * masks[j], kk, _NT) * (HEAD_DIM ** -0.5)
                s = jnp.where(ok, s, NEG_BIG)
                sink = sink_ref[j * (N_Q_C // N_KV_C) + g]
                m = jnp.maximum(jnp.max(s, axis=-1, keepdims=True), sink)
                e = jnp.exp(s - m)
                den = jnp.sum(e, axis=-1, keepdims=True) + jnp.exp(sink - m)
                og = og + _mm3(e / den, vv * masks[j])
            outs.append(og)
        y_ref[sl, :] = jnp.concatenate(outs, axis=-1)
        kprev[...] = ki
        vprev[...] = vi
    klast_ref[...] = k[tb - WINDOW:tb, :]
    vlast_ref[...] = v[tb - WINDOW:tb, :]


def _swa_prompt(z, nb, t_len, p, cos_t, sin_t, tb=512):
    nt = t_len // tb
    sd = lambda *s: jax.ShapeDtypeStruct(s, F32)
    return pl.pallas_call(
        functools.partial(_swa_prompt_body, tb=tb),
        grid=(nb, nt),
        in_specs=[pl.BlockSpec(memory_space=pltpu.SMEM),
                  pl.BlockSpec((tb, 4 * LANES), lambda b, t: (b * nt + t, 2)),
                  pl.BlockSpec((tb, LANES), lambda b, t: (t, 0)),
                  pl.BlockSpec((tb, LANES), lambda b, t: (t, 0)),
                  _const_spec((1, 2 * LANES)), _const_spec((1, LANES))],
        out_specs=[pl.BlockSpec((tb, 2 * LANES), lambda b, t: (b * nt + t, 0)),
                   pl.BlockSpec((None, WINDOW, LANES), lambda b, t: (b, 0, 0)),
                   pl.BlockSpec((None, WINDOW, LANES), lambda b, t: (b, 0, 0))],
        out_shape=[sd(nb * t_len, 2 * LANES), sd(nb, WINDOW, LANES), sd(nb, WINDOW, LANES)],
        scratch_shapes=[pltpu.VMEM((WINDOW, LANES), F32), pltpu.VMEM((WINDOW, LANES), F32)],
        compiler_params=_cparams(("arbitrary", "arbitrary")),
        name="swa_prompt",
    )(p["sinks"], z, cos_t, sin_t, p["q_gain"], p["k_gain"])


def _swa_sample_body(z_ref, cos_ref, sin_ref, qg_ref, kg_ref, sink_ref, kb_ref, vb_ref,
                     y_ref, knew_ref, *, bb, nt):
    seg = _seg_ones(LANES, HEAD_DIM)
    kb = kb_ref[...]
    vb = vb_ref[...]
    nkeys = kb.shape[0]
    kidx = lax.broadcasted_iota(jnp.int32, kb.shape, 0)
    qs, ks, vs = [], [], []
    for t in range(nt):
        zt = z_ref[t]
        cos, sin = cos_ref[t], sin_ref[t]
        q = _head_rms(zt[:, 0:2 * LANES], qg_ref[...])
        qs.append(_rope(q, jnp.concatenate([cos, cos], axis=-1), jnp.concatenate([sin, sin], axis=-1)))
        kt = _rope(_head_rms(zt[:, 2 * LANES:3 * LANES], kg_ref[...]), cos, sin)
        ks.append(kt)
        vs.append(zt[:, 3 * LANES:4 * LANES])
        knew_ref[t] = kt
    scale = HEAD_DIM ** -0.5
    for t in range(nt):
        outs = []
        for g in range(N_Q_C // N_KV_C):
            qtg = qs[t][:, g * LANES:(g + 1) * LANES]
            prod = (qtg[None, :, :] * kb).reshape(nkeys * bb, LANES)
            sb = (_mm_x01(prod, seg) * scale).reshape(nkeys, bb, LANES)
            sb = jnp.where(kidx >= t + 1 + (nkeys - WINDOW), sb, NEG_BIG)
            sn = [_mm_x01(qtg * ks[s], seg) * scale for s in range(t + 1)]
            sink = sink_ref[g]
            m = jnp.maximum(jnp.max(sb, axis=0), sink)
            for x in sn:
                m = jnp.maximum(m, x)
            eb = jnp.exp(sb - m[None, :, :])
            den = jnp.sum(eb, axis=0) + jnp.exp(sink - m)
            num = jnp.sum(eb * vb, axis=0)
            for s, x in enumerate(sn):
                en = jnp.exp(x - m)
                den = den + en
                num = num + en * vs[s]
            outs.append(num / den)
        y_ref[t] = jnp.concatenate(outs, axis=-1)


def _swa_sample(z3, cos_s, sin_s, p, kb_t, vb_t, nb, nt, bb=16):
    sd = lambda *s: jax.ShapeDtypeStruct(s, F32)
    nk = kb_t.shape[0]
    c3 = lambda shape: pl.BlockSpec(shape, lambda i: (0,) * len(shape))
    return pl.pallas_call(
        functools.partial(_swa_sample_body, bb=bb, nt=nt),
        grid=(nb // bb,),
        in_specs=[pl.BlockSpec((nt, bb, 4 * LANES), lambda i: (0, i, 2)),
                  c3((nt, 1, LANES)), c3((nt, 1, LANES)), c3((1, 2 * LANES)), c3((1, LANES)),
                  c3((N_Q_C // N_KV_C, 1, LANES)),
                  pl.BlockSpec((nk, bb, LANES), lambda i: (0, i, 0)),
                  pl.BlockSpec((nk, bb, LANES), lambda i: (0, i, 0))],
        out_specs=[pl.BlockSpec((nt, bb, 2 * LANES), lambda i: (0, i, 0)),
                   pl.BlockSpec((nt, bb, LANES), lambda i: (0, i, 0))],
        out_shape=[sd(nt, nb, 2 * LANES), sd(nt, nb, LANES)],
        compiler_params=_cparams(("arbitrary",)),
        name="swa_sample",
    )(z3, cos_s, sin_s, p["q_gain"], p["k_gain"], p["sink_lanes"], kb_t, vb_t)


def _hgrn_body(*refs, rows, has_init, n_valid):
    if has_init:
        (zqf_ref, zig_ref, lb_ref, gn_ref, s0_ref, y_ref, sfin_ref,
         st, qq_s, qe_s, kk_s, kd_s, bc_s, bl_s, v_s, o_s) = refs
    else:
        (zqf_ref, zig_ref, lb_ref, gn_ref, y_ref, sfin_ref,
         st, qq_s, qe_s, kk_s, kd_s, bc_s, bl_s, v_s, o_s) = refs
        s0_ref = None
    t = pl.program_id(1)

    @pl.when(t == 0)
    def _():
        if has_init:
            st[...] = s0_ref[...]
        else:
            st[...] = jnp.zeros_like(st)

    sup = HG_SUPER
    n_super = max(rows // sup, 1)
    real = min(rows, sup)
    lb = lb_ref[...]
    seg_heads = _seg_ones(W_GROUP, HEAD_DIM)
    bd_mask = seg_heads.astype(F32)
    ri = lax.broadcasted_iota(jnp.int32, (sup, sup), 0)
    ci = lax.broadcasted_iota(jnp.int32, (sup, sup), 1)
    same = (ri // HG_SUB) == (ci // HG_SUB)
    tri_bd = jnp.where(same & (ci <= ri), 1.0, 0.0).astype(BF)
    ones_bd = jnp.where(same, 1.0, 0.0).astype(BF)
    row_sup = lax.broadcasted_iota(jnp.int32, (sup, W_GROUP), 0)
    pr = lax.broadcasted_iota(jnp.int32, (HG_SUB * HG_SUB, W_GROUP), 0)
    causal = jnp.where((pr % HG_SUB) >= (pr // HG_SUB), 1.0, 0.0)

    def pad(x):
        if real == sup:
            return x
        return jnp.concatenate([x, jnp.zeros((sup - real, x.shape[1]), F32)], axis=0)

    for sb in range(n_super):
        rs = slice(sb * sup, sb * sup + real)
        zf = zqf_ref[rs, W_GROUP:2 * W_GROUP]
        f = lb + (1.0 - lb) * _sigmoid(zf)
        logf = jnp.log(jnp.maximum(f, 1e-30))
        kk = (1.0 - lb) * _sigmoid(-zf)
        if n_valid < real:
            valid = lax.broadcasted_iota(jnp.int32, (real, W_GROUP), 0) < n_valid
            logf = jnp.where(valid, logf, 0.0)
            kk = jnp.where(valid, kk, 0.0)
        logf, kk = pad(logf), pad(kk)
        qq = pad(_silu(zqf_ref[rs, 0:W_GROUP]))
        v = pad(zig_ref[rs, 0:W_GROUP])
        bc = _mm_01x(tri_bd, logf)
        bl = _mm_01x(ones_bd, logf)
        qq_s[...] = qq
        kk_s[...] = kk
        bc_s[...] = bc
        bl_s[...] = bl
        qe_s[...] = qq * jnp.exp(bc)
        kd = kk * jnp.exp(bl - bc)
        v_s[...] = v
        vt = v.T

        n_sub = -(-real // HG_SUB)

        def sub(c, carry):
            r0 = pl.multiple_of(c * HG_SUB, HG_SUB)
            rsub = pl.ds(r0, HG_SUB)
            bc_c = bc_s[rsub, :]
            qq_c = qq_s[rsub, :]
            kk_c = kk_s[rsub, :]
            v_c = v_s[rsub, :]
            parts = []
            for s in range(HG_SUB):
                e = jnp.exp(jnp.minimum(bc_c - bc_c[s:s + 1, :], 0.0))
                parts.append(qq_c * e * kk_c[s:s + 1, :])
            pm = jnp.concatenate(parts, axis=0) * causal
            att = _mm_x01(pm, seg_heads)
            o_c = _mm3(qe_s[rsub, :], st[...], _NT)
            for s in range(HG_SUB):
                o_c = o_c + att[s * HG_SUB:(s + 1) * HG_SUB, :] * v_c[s:s + 1, :]
            o_s[pl.ds(pl.multiple_of(sb * sup + r0, HG_SUB), HG_SUB), :] = o_c
            kd_c = jnp.where((row_sup // HG_SUB) == c, kd, 0.0)
            decay = jnp.exp(bl_s[pl.ds(r0, 1), :])
            st[...] = st[...] * decay + _mm3(vt, kd_c) * bd_mask
            return carry

        lax.fori_loop(0, n_sub, sub, 0)

    o = o_s[0:rows, :]
    zg = zig_ref[:, W_GROUP:2 * W_GROUP]
    y_ref[...] = _head_rms(o, gn_ref[...]) * _silu(zg)
    sfin_ref[...] = st[...]


def _hgrn(zqf_src, zig_src, qf_col, ig_col, nseq, t_len, rows, p, s0=None, n_valid=None):
    nt = t_len // rows
    n_valid = rows if n_valid is None else n_valid
    sd = lambda *s: jax.ShapeDtypeStruct(s, F32)
    in_specs = [pl.BlockSpec((rows, 2 * W_GROUP), lambda b, t: (b * nt + t, qf_col)),
                pl.BlockSpec((rows, 2 * W_GROUP), lambda b, t: (b * nt + t, ig_col)),
                _const_spec((1, W_GROUP)), _const_spec((1, W_GROUP))]
    args = [zqf_src, zig_src, p["hgrn_lb"], p["hgrn_gn"]]
    if s0 is not None:
        in_specs.append(pl.BlockSpec((None, W_GROUP, W_GROUP), lambda b, t: (b, 0, 0)))
        args.append(s0)
    scr_rows = max(rows, HG_SUPER)
    sup_shape = pltpu.VMEM((HG_SUPER, W_GROUP), F32)
    return pl.pallas_call(
        functools.partial(_hgrn_body, rows=rows, has_init=s0 is not None, n_valid=n_valid),
        grid=(nseq, nt),
        in_specs=in_specs,
        out_specs=[pl.BlockSpec((rows, W_GROUP), lambda b, t: (b * nt + t, 0)),
                   pl.BlockSpec((None, W_GROUP, W_GROUP), lambda b, t: (b, 0, 0))],
        out_shape=[sd(nseq * t_len, W_GROUP), sd(nseq, W_GROUP, W_GROUP)],
        scratch_shapes=[pltpu.VMEM((W_GROUP, W_GROUP), F32)] + [sup_shape] * 7
        + [pltpu.VMEM((scr_rows, W_GROUP), F32)],
        compiler_params=_cparams(("arbitrary", "arbitrary")),
        name="hgrn",
    )(*args)


def _oddeven_merge_sort(n):
    pairs = []

    def merge(lo, m, r):
        step = r * 2
        if step < m:
            merge(lo, m, step)
            merge(lo + r, m, step)
            for i in range(lo + r, lo + m - r, step):
                pairs.append((i, i + r))
        else:
            pairs.append((lo, lo + r))

    def sort(lo, m):
        if m > 1:
            h = m // 2
            sort(lo, h)
            sort(lo + h, h)
            merge(lo, m, 1)

    sort(0, n)
    return pairs


_NET16 = _oddeven_merge_sort(PEER_TOPK)
_CELLS = [(a, b) for a in range(PEER_TOPK) for b in range(PEER_TOPK)
          if (a + 1) * (b + 1) <= PEER_TOPK]


def _beats(va, ia, vb, ib):
    return (va > vb) | ((va == vb) & (ia < ib))


def _cmpx(items, i, j):
    (va, ia), (vb, ib) = items[i], items[j]
    gt = _beats(va, ia, vb, ib)
    items[i] = (jnp.where(gt, va, vb), jnp.where(gt, ia, ib))
    items[j] = (jnp.where(gt, vb, va), jnp.where(gt, ib, ia))


def _merge_top(a_items, b_items):
    n = len(a_items)
    items = []
    for i in range(n):
        (va, ia), (vb, ib) = a_items[i], b_items[n - 1 - i]
        gt = _beats(va, ia, vb, ib)
        items.append((jnp.where(gt, va, vb), jnp.where(gt, ia, ib)))
    d = n // 2
    while d >= 1:
        for i in range(n):
            if not (i & d):
                _cmpx(items, i, i + d)
        d //= 2
    return items


def _top16_sorted(load_key, n_keys, shape):
    stack = []
    for g in range(n_keys // PEER_TOPK):
        items = [(load_key(g * PEER_TOPK + j), jnp.full(shape, float(g * PEER_TOPK + j), F32))
                 for j in range(PEER_TOPK)]
        for i, j in _NET16:
            _cmpx(items, i, j)
        level = 0
        while stack and stack[-1][0] == level:
            _, other = stack.pop()
            items = _merge_top(other, items)
            level += 1
        stack.append((level, items))
    assert len(stack) == 1
    return stack[0][1]


def _route_body(h_ref, wqh_ref, wql_ref, kbh_ref, kbl_ref, e1_ref, e2_ref, g_ref,
                st_s, top_s, *, tn):
    hh, hl = _hi_lo(h_ref[...])
    wqh, wql = wqh_ref[...], wql_ref[...]
    qt = (_dg(wqh, hl, _NT) + _dg(wql, hh, _NT)) + _dg(wqh, hh, _NT)
    half = PEER_HEADS * PEER_DQ // 2
    for p in range(2):
        qp = qt[p * half:(p + 1) * half, :]
        qh, ql = _hi_lo(qp)
        kh, kl = kbh_ref[p], kbl_ref[p]
        st_s[p] = (_dg(kh, ql, _NN) + _dg(kl, qh, _NN)) + _dg(kh, qh, _NN)

    shape = (PEER_HEADS, LANES)

    def column(col, carry):
        c0 = pl.multiple_of(col * LANES, LANES)
        cs = pl.ds(c0, LANES)
        for p in range(2):
            top = _top16_sorted(lambda k: st_s[p, k * PEER_HEADS:(k + 1) * PEER_HEADS, cs],
                                PEER_KEYS, shape)
            for a, (va, ia) in enumerate(top):
                top_s[p, 0, a] = va
                top_s[p, 1, a] = ia
        v1 = [top_s[0, 0, a] for a in range(PEER_TOPK)]
        v2 = [top_s[1, 0, b] for b in range(PEER_TOPK)]
        cval = {cell: v1[cell[0]] + v2[cell[1]] for cell in _CELLS}
        rank_static = {cell: float((cell[0] + 1) * (cell[1] + 1) - 1) for cell in _CELLS}
        dyn = {cell: None for cell in _CELLS}
        for xi, x in enumerate(_CELLS):
            for y in _CELLS[xi + 1:]:
                if x[0] < y[0] and x[1] > y[1]:
                    xw = jnp.where(cval[x] >= cval[y], 1.0, 0.0)
                    dyn[y] = xw if dyn[y] is None else dyn[y] + xw
                    lose = 1.0 - xw
                    dyn[x] = lose if dyn[x] is None else dyn[x] + lose
        rank = {cell: (rank_static[cell] if dyn[cell] is None else dyn[cell] + rank_static[cell])
                for cell in _CELLS}
        c00 = cval[(0, 0)]
        ex = {}
        zsum = None
        for cell in _CELLS:
            if dyn[cell] is None:
                w = jnp.exp(cval[cell] - c00)
            else:
                w = jnp.where(rank[cell] < PEER_TOPK, jnp.exp(cval[cell] - c00), 0.0)
            ex[cell] = w
            zsum = w if zsum is None else zsum + w
        inv = 1.0 / zsum
        for k in range(PEER_TOPK):
            e1 = jnp.zeros(shape, F32)
            e2 = jnp.zeros(shape, F32)
            gk = jnp.zeros(shape, F32)
            for cell in _CELLS:
                lo = (cell[0] + 1) * (cell[1] + 1) - 1
                if lo > k:
                    continue
                if dyn[cell] is None:
                    if lo != k:
                        continue
                    e1, e2, gk = top_s[0, 1, cell[0]], top_s[1, 1, cell[1]], ex[cell]
                    continue
                hit = rank[cell] == float(k)
                e1 = jnp.where(hit, top_s[0, 1, cell[0]], e1)
                e2 = jnp.where(hit, top_s[1, 1, cell[1]], e2)
                gk = jnp.where(hit, ex[cell], gk)
            rows = slice(k * PEER_HEADS, (k + 1) * PEER_HEADS)
            e1_ref[rows, cs] = e1
            e2_ref[rows, cs] = e2
            g_ref[rows, cs] = gk * inv
        return carry

    lax.fori_loop(0, tn // LANES, column, 0)


def _route(h2, p, tn=256):
    n = h2.shape[0]
    nslots = PEER_TOPK * PEER_HEADS
    half = PEER_HEADS * PEER_DQ // 2
    nk = PEER_KEYS * PEER_HEADS
    sd = jax.ShapeDtypeStruct((nslots, n), F32)
    c1 = lambda shape: pl.BlockSpec(shape, lambda i: (0,) * len(shape))
    return pl.pallas_call(
        functools.partial(_route_body, tn=tn),
        grid=(n // tn,),
        in_specs=[pl.BlockSpec((tn, D_MODEL), lambda i: (i, 0)),
                  c1((2 * half, D_MODEL)), c1((2 * half, D_MODEL)),
                  c1((2, nk, half)), c1((2, nk, half))],
        out_specs=[pl.BlockSpec((nslots, tn), lambda i: (0, i))] * 3,
        out_shape=[sd, sd, sd],
        scratch_shapes=[pltpu.VMEM((2, nk, tn), F32),
                        pltpu.VMEM((2, 2, PEER_TOPK, PEER_HEADS, LANES), F32)],
        compiler_params=_cparams(("arbitrary",)),
        name="peer_route",
    )(h2, p["wq_t_hi"], p["wq_t_lo"], p["kbig_hi"], p["kbig_lo"])


PEER_CPAIR = 2 * PEER_KEYS


def _peer_body(h_ref, e1_ref, e2_ref, g_ref, u_ref, v_ref, x1_ref, g2_ref, o_ref,
               gt_s, xb_s, acc_s, *, tn):
    cp = pl.program_id(2)

    @pl.when(cp == 0)
    def _():
        xb_s[...] = h_ref[...].astype(BF)
        acc_s[...] = jnp.zeros_like(acc_s)
        sub = lax.broadcasted_iota(jnp.int32, (PEER_KEYS, LANES), 0).astype(F32)

        def build(n, carry):
            e1 = e1_ref[pl.ds(n, 1), :]
            e2 = e2_ref[pl.ds(n, 1), :]
            g = g_ref[pl.ds(n, 1), :]
            pt = jnp.where(sub == e1, 1.0, 0.0).astype(BF)
            qg = jnp.where(sub == e2, g, 0.0).astype(BF)
            gt_s[pl.ds(pl.multiple_of(n * PEER_KEYS, PEER_KEYS), PEER_KEYS), :] = _dg(pt, qg, _NT)
            return carry

        lax.fori_loop(0, tn, build, 0)

    a = _gelu(_dg(xb_s[...], u_ref[...], _NT))
    g0 = gt_s[pl.ds(2 * cp, tn, stride=PEER_KEYS), :]
    g1 = gt_s[pl.ds(2 * cp + 1, tn, stride=PEER_KEYS), :]
    ga = (a * jnp.concatenate([g0, g1], axis=-1)).astype(BF)
    acc_s[...] += jnp.dot(ga, v_ref[...], preferred_element_type=F32)

    @pl.when(cp == pl.num_programs(2) - 1)
    def _():
        o_ref[...] = x1_ref[...] + g2_ref[...] * acc_s[...]


def _peer(rows, h2, e1, e2, g, u_bf, v_bf, x1, mod):
    tn = rows.tn
    nt = rows.nt
    ncp = PEER_KEYS * PEER_KEYS // PEER_CPAIR
    tok = lambda width: pl.BlockSpec((tn, width), lambda b, t, c: (b * nt + t, 0))
    if rows.per_row_mod:
        g2_spec = pl.BlockSpec((tn, D_MODEL), lambda b, t, c: (b * nt + t, 5))
    else:
        g2_spec = pl.BlockSpec((None, 1, D_MODEL), lambda b, t, c: (b, 0, 5))
    return pl.pallas_call(
        functools.partial(_peer_body, tn=tn),
        grid=(rows.nb, rows.nt, ncp),
        in_specs=[tok(D_MODEL), tok(LANES), tok(LANES), tok(LANES),
                  pl.BlockSpec((PEER_CPAIR, D_MODEL), lambda b, t, c: (c, 0)),
                  pl.BlockSpec((PEER_CPAIR, D_MODEL), lambda b, t, c: (c, 0)),
                  tok(D_MODEL), g2_spec],
        out_specs=tok(D_MODEL),
        out_shape=jax.ShapeDtypeStruct((rows.rows, D_MODEL), F32),
        scratch_shapes=[pltpu.VMEM((tn * PEER_KEYS, LANES), F32),
                        pltpu.VMEM((tn, D_MODEL), BF),
                        pltpu.VMEM((tn, D_MODEL), F32)],
        compiler_params=_cparams(("arbitrary", "arbitrary", "arbitrary")),
        name="peer_experts",
    )(h2, e1, e2, g, u_bf, v_bf, x1, mod)


_Q_HEAD_ORDER = (0, 2, 1, 3)


def _q_perm():
    return np.concatenate([np.arange(h * HEAD_DIM, (h + 1) * HEAD_DIM) for h in _Q_HEAD_ORDER])


def _block_diag(w):
    nblk, bi, bo = w.shape
    out = jnp.zeros((nblk * bi, nblk * bo), w.dtype)
    for h in range(nblk):
        out = out.at[h * bi:(h + 1) * bi, h * bo:(h + 1) * bo].set(w[h])
    return out


def _rope_tables(pos):
    half = HEAD_DIM // 2
    freqs = ROPE_THETA ** (-jnp.arange(half, dtype=F32) / half)
    ang = pos.astype(F32)[:, None] * freqs[None, :]
    cos, sin = jnp.cos(ang), jnp.sin(ang)
    cos_h = jnp.concatenate([cos, cos], axis=-1)
    sin_h = jnp.concatenate([-sin, sin], axis=-1)
    reps = LANES // HEAD_DIM
    return jnp.tile(cos_h, (1, reps)), jnp.tile(sin_h, (1, reps))


def _layer_params(i, nt_s, lower_bound, w_in, conv_w, conv_b, lru_wa, lru_ba, lru_wx, lru_bx,
                  lru_lambda, sgu_norm, sgu_ws, sgu_b, q_norm, k_norm, sinks, hgrn_gnorm,
                  out_norm, w_out, norm_ffn, norm_mix, peer_wq, peer_keys, peer_u, peer_v):
    row = lambda v: v.reshape(1, -1)
    qperm = _q_perm()
    q0 = 4 * W_GROUP
    in_perm = np.arange(D_IN)
    in_perm[q0:q0 + W_GROUP] = q0 + qperm
    c0 = 2 * W_GROUP
    mix_perm = np.arange(N_GROUPS * W_GROUP)
    mix_perm[c0:c0 + W_GROUP] = c0 + qperm
    nh = W_GROUP // HEAD_DIM
    causal = jnp.tril(jnp.ones((CHUNK_B, CHUNK_B), F32))
    ws_c = sgu_ws[i] * causal
    kbig = jnp.zeros((2, PEER_KEYS, PEER_HEADS, PEER_HEADS, PEER_DQ // 2), F32)
    for h in range(PEER_HEADS):
        kbig = kbig.at[:, :, h, h, :].set(peer_keys[i, h])
    kbig = kbig.reshape(2, PEER_KEYS * PEER_HEADS, PEER_HEADS * PEER_DQ // 2)
    wq_t = peer_wq[i].reshape(D_MODEL, PEER_HEADS, 2, PEER_DQ // 2)
    wq_t = wq_t.transpose(2, 1, 3, 0).reshape(PEER_HEADS * PEER_DQ, D_MODEL)
    wq_hi, wq_lo = _hi_lo(wq_t)
    kb_hi, kb_lo = _hi_lo(kbig)
    sink_q = sinks[i]
    sink_lanes = jnp.stack([
        jnp.concatenate([jnp.full((HEAD_DIM,), sink_q[j * 2 + g]) for j in range(N_KV_C)])
        for g in range(N_Q_C // N_KV_C)]).reshape(N_Q_C // N_KV_C, 1, LANES)
    return {
        "norm_mix": row(norm_mix[i]),
        "w_in": w_in[i][:, in_perm].astype(BF),
        "conv_w": conv_w[i], "conv_b": row(conv_b[i]),
        "wa_bd": _block_diag(lru_wa[i]), "wx_bd": _block_diag(lru_wx[i]),
        "lru_ba": row(lru_ba[i]), "lru_bx": row(lru_bx[i]), "lru_lambda": row(lru_lambda[i]),
        "sgu_norm": row(sgu_norm[i]),
        "sgu_ws_cat": jnp.concatenate([ws_c[g] for g in range(nh)], axis=1),
        "sgu_bias": jnp.repeat(sgu_b[i].T, HEAD_DIM, axis=1),
        "sgu_mw_s": jnp.repeat(ws_c[:, :nt_s, :nt_s].transpose(1, 2, 0), HEAD_DIM,
                               axis=-1).reshape(nt_s, nt_s, 1, W_GROUP),
        "sgu_mb_s": jnp.repeat(sgu_b[i][:, :nt_s].T, HEAD_DIM, axis=-1).reshape(nt_s, 1, W_GROUP),
        "q_gain": row(jnp.tile(q_norm[i], N_Q_C)), "k_gain": row(jnp.tile(k_norm[i], N_KV_C)),
        "sinks": sink_q, "sink_lanes": sink_lanes,
        "hgrn_lb": row(lower_bound[i]), "hgrn_gn": row(jnp.tile(hgrn_gnorm[i], HGRN_HEADS)),
        "out_norm": row(out_norm[i][mix_perm]),
        "w_out": w_out[i][mix_perm, :].astype(BF),
        "norm_ffn": row(norm_ffn[i]),
        "wq_t_hi": wq_hi, "wq_t_lo": wq_lo, "kbig_hi": kb_hi, "kbig_lo": kb_lo,
        "peer_u": peer_u[i].astype(BF), "peer_v": peer_v[i].astype(BF),
    }


def _state_to_bd_t(s):
    b = s.shape[0]
    st = jnp.zeros((b, HGRN_HEADS, HEAD_DIM, HGRN_HEADS, HEAD_DIM), F32)
    for h in range(HGRN_HEADS):
        st = st.at[:, h, :, h, :].set(jnp.swapaxes(s[:, h], 1, 2))
    return st.reshape(b, W_GROUP, W_GROUP)


def _bd_t_to_state(st):
    b = st.shape[0]
    s5 = st.reshape(b, HGRN_HEADS, HEAD_DIM, HGRN_HEADS, HEAD_DIM)
    return jnp.stack([jnp.swapaxes(s5[:, h, :, h, :], 1, 2) for h in range(HGRN_HEADS)], axis=1)


def _peer_block(rows, h2, x1, mod, p):
    e1t, e2t, gt = _route(h2, p)
    return _peer(rows, h2, e1t.T, e2t.T, gt.T, p["peer_u"], p["peer_v"], x1, mod)


def kernel(x_prompt, x_sample, state_lru_h, state_lru_conv, state_swa_k, state_swa_v, state_hgrn_S,
           c_prompt, c_sample, w_ada, b_ada, norm_mix, w_in, conv_w, conv_b, lru_wa, lru_ba, lru_wx,
           lru_bx, lru_lambda, sgu_norm, sgu_ws, sgu_b, q_norm, k_norm, sinks, hgrn_lb, hgrn_gnorm,
           out_norm, w_out, norm_ffn, peer_wq, peer_keys, peer_u, peer_v):
    bp, tp, _ = x_prompt.shape
    bs, ts, _ = x_sample.shape
    n_s = bs * ts
    win = state_swa_k.shape[2]

    lbp = jax.nn.softmax(hgrn_lb.astype(F32), axis=0)
    lower_bound = jnp.cumsum(lbp, axis=0) - lbp[0]

    c_all = jnp.concatenate([c_prompt, c_sample], axis=0)
    pad_c = (-c_all.shape[0]) % SUBLANES
    mods = _ada(jnp.pad(c_all, ((0, pad_c), (0, 0))), w_ada, b_ada)

    cos_p, sin_p = _rope_tables(jnp.arange(tp, dtype=jnp.int32))
    cos_s, sin_s = _rope_tables(PAST_LEN + jnp.arange(ts, dtype=jnp.int32))
    cos_s, sin_s = cos_s.reshape(ts, 1, LANES), sin_s.reshape(ts, 1, LANES)

    rows_p = _Rows(bp, tp // 512, 512, per_row_mod=False)
    rows_s = _Rows(1, 1, n_s, per_row_mod=True)
    peer_rows_p = _Rows(bp, tp // 256, 256, per_row_mod=False)
    peer_rows_s = _Rows(1, n_s // 256, 256, per_row_mod=True)

    xp = x_prompt.reshape(bp * tp, D_MODEL)
    xs = jnp.swapaxes(x_sample, 0, 1).reshape(n_s, D_MODEL)
    st_p, st_s = [], []
    for i in range(DEPTH):
        p = _layer_params(i, ts, lower_bound, w_in, conv_w, conv_b, lru_wa, lru_ba, lru_wx, lru_bx,
                          lru_lambda, sgu_norm, sgu_ws, sgu_b, q_norm, k_norm, sinks, hgrn_gnorm,
                          out_norm, w_out, norm_ffn, norm_mix, peer_wq, peer_keys, peer_u, peer_v)
        mod_p = mods[i, :bp].reshape(bp, 1, 6 * D_MODEL)
        mod_s = jnp.tile(mods[i, bp:bp + bs], (ts, 1))

        z = _in_proj(rows_p, xp, mod_p, p["norm_mix"], p["w_in"])
        ya, h_fin, tail = _lru_prompt(z, bp, tp, p)
        yb = _sgu_prompt(z, bp, tp, p)
        yc, k_last, v_last = _swa_prompt(z, bp, tp, p, cos_p, sin_p)
        yd, s_fin = _hgrn(z, z, 3, 4, bp, tp, 512, p)
        x1, h2 = _out_proj(rows_p, xp, (ya, yb, yc, yd), mod_p, p["out_norm"], p["w_out"], p["norm_ffn"])
        xp = _peer_block(peer_rows_p, h2, x1, mod_p, p)
        st_p.append((h_fin.reshape(bp, W_GROUP), tail[:, SUBLANES - (CONV_W - 1):, :],
                     k_last.reshape(bp, WINDOW, N_KV_C, HEAD_DIM),
                     v_last.reshape(bp, WINDOW, N_KV_C, HEAD_DIM), _bd_t_to_state(s_fin)))

        zs = _in_proj(rows_s, xs, mod_s, p["norm_mix"], p["w_in"])
        ya, h_fin, nbuf, yb, vrows = _ab_sample(
            zs, state_lru_h[:, i], jnp.swapaxes(state_lru_conv[:, i], 0, 1), p, bs, ts)
        kb = state_swa_k[:, i].reshape(bs, win, LANES)
        vb = state_swa_v[:, i].reshape(bs, win, LANES)
        yc3, knew = _swa_sample(zs.reshape(ts, bs, D_IN), cos_s, sin_s, p,
                                jnp.swapaxes(kb, 0, 1), jnp.swapaxes(vb, 0, 1), bs, ts)
        vnew = zs.reshape(ts, bs, D_IN)[:, :, 4 * W_GROUP + 3 * LANES:4 * W_GROUP + 4 * LANES]
        zd = jnp.swapaxes(zs.reshape(ts, bs, D_IN)[:, :, 6 * W_GROUP:], 0, 1)
        zd = jnp.pad(zd, ((0, 0), (0, HG_SUB - ts), (0, 0))).reshape(bs * HG_SUB, 4 * W_GROUP)
        yd_pad, s_fin = _hgrn(zd, zd, 0, 1, bs, HG_SUB, HG_SUB, p,
                              s0=_state_to_bd_t(state_hgrn_S[:, i]), n_valid=ts)
        yd = jnp.swapaxes(yd_pad.reshape(bs, HG_SUB, W_GROUP)[:, :ts], 0, 1).reshape(n_s, W_GROUP)
        x1, h2 = _out_proj(rows_s, xs, (ya, yb, yc3.reshape(n_s, 2 * LANES), yd), mod_s,
                           p["out_norm"], p["w_out"], p["norm_ffn"])
        xs = _peer_block(peer_rows_s, h2, x1, mod_s, p)
        k_win = jnp.concatenate([kb, jnp.swapaxes(knew, 0, 1)], axis=1)[:, -win:]
        v_win = jnp.concatenate([vb, jnp.swapaxes(vnew, 0, 1)], axis=1)[:, -win:]
        st_s.append((h_fin, jnp.swapaxes(nbuf, 0, 1),
                     k_win.reshape(bs, win, N_KV_C, HEAD_DIM), v_win.reshape(bs, win, N_KV_C, HEAD_DIM),
                     _bd_t_to_state(s_fin),
                     jnp.swapaxes(vrows.reshape(ts, bs, W_GROUP), 0, 1)))

    stack = lambda per_layer, j: jnp.stack([s[j] for s in per_layer], axis=1)
    y_p = xp.reshape(bp, tp, D_MODEL)
    y_s = jnp.swapaxes(xs.reshape(ts, bs, D_MODEL), 0, 1)
    return (y_p, y_s,
            stack(st_p, 0), stack(st_p, 1), stack(st_p, 2), stack(st_p, 3), stack(st_p, 4),
            stack(st_s, 0), stack(st_s, 1), stack(st_s, 2), stack(st_s, 3), stack(st_s, 4),
            stack(st_s, 5))
```

```python
import functools

import numpy as np
import jax
import jax.numpy as jnp
from jax import lax
from jax.experimental import pallas as pl
from jax.experimental.pallas import tpu as pltpu

F32 = jnp.float32
BF = jnp.bfloat16

D_MODEL = 1024
DEPTH = 2
PAST_LEN = 16384
HEAD_DIM = 64
W_GROUP = 256
N_GROUPS = 4
CONV_W = 4
LRU_C = 8.0
LRU_BLOCKS = 4
LRU_FLOOR = 1e-12
CHUNK_B = 128
WINDOW = 128
N_Q_C = 4
N_KV_C = 2
ROPE_THETA = 10000.0
NEG_BIG = -1e30
HGRN_HEADS = 4
PEER_HEADS = 8
PEER_KEYS = 128
PEER_DQ = 128
PEER_TOPK = 16
EPS = 1e-6
D_IN = 2560

LANES = 128
SUBLANES = 8
VMEM_LIMIT = 56 * 1024 * 1024

HG_SUB = 16
HG_SUPER = 128


def _cparams(sem):
    return pltpu.CompilerParams(dimension_semantics=sem, vmem_limit_bytes=VMEM_LIMIT)


_NN = (((1,), (0,)), ((), ()))
_NT = (((1,), (1,)), ((), ()))


def _dg(a, b, dims):
    return lax.dot_general(a, b, dims, preferred_element_type=F32)


def _hi_lo(a):
    hi = a.astype(BF)
    lo = (a - hi.astype(F32)).astype(BF)
    return hi, lo


def _mm3(a, b, dims=_NN):
    ah, al = _hi_lo(a)
    bh, bl = _hi_lo(b)
    return (_dg(ah, bl, dims) + _dg(al, bh, dims)) + _dg(ah, bh, dims)


def _split3(a):
    a1 = a.astype(BF)
    r1 = a - a1.astype(F32)
    a2 = r1.astype(BF)
    a3 = (r1 - a2.astype(F32)).astype(BF)
    return a1, a2, a3


def _mm_x01(a, b01, dims=_NN):
    a1, a2, a3 = _split3(a)
    return (_dg(a3, b01, dims) + _dg(a2, b01, dims)) + _dg(a1, b01, dims)


def _mm_01x(a01, b, dims=_NN):
    b1, b2, b3 = _split3(b)
    return (_dg(a01, b3, dims) + _dg(a01, b2, dims)) + _dg(a01, b1, dims)


def _sigmoid(x):
    return jax.nn.sigmoid(x)


def _silu(x):
    return x * jax.nn.sigmoid(x)


def _gelu(x):
    return x * (0.5 * (1.0 + jnp.tanh(0.7978845608028654 * (x + 0.044715 * (x * x * x)))))


def _rms(x, w):
    return x * lax.rsqrt(jnp.mean(x * x, axis=-1, keepdims=True) + EPS) * w


def _seg_ones(width, seg):
    r = lax.broadcasted_iota(jnp.int32, (width, width), 0) // seg
    c = lax.broadcasted_iota(jnp.int32, (width, width), 1) // seg
    return jnp.where(r == c, 1.0, 0.0).astype(BF)


def _head_rms(x, gain):
    ms = _mm_x01(x * x, _seg_ones(x.shape[-1], HEAD_DIM)) * (1.0 / HEAD_DIM)
    return x * lax.rsqrt(ms + EPS) * gain


def _rope(x, cos_f, sin_s):
    w = x.shape[-1]
    lane = lax.broadcasted_iota(jnp.int32, x.shape, x.ndim - 1) % HEAD_DIM
    rot = jnp.where(lane < HEAD_DIM // 2,
                    pltpu.roll(x, w - HEAD_DIM // 2, x.ndim - 1),
                    pltpu.roll(x, HEAD_DIM // 2, x.ndim - 1))
    return x * cos_f + rot * sin_s


def _lane_head_mask(width, j):
    lane = lax.broadcasted_iota(jnp.int32, (1, width), 1) // HEAD_DIM
    return jnp.where(lane == j, 1.0, 0.0)


def _ada_body(c_ref, w_ref, b_ref, o_ref):
    c = _silu(c_ref[...])
    o_ref[...] = _mm3(c, w_ref[...]) + b_ref[...]


def _ada(c_all, w_ada, b_ada):
    n = c_all.shape[0]
    cb = 1536
    return pl.pallas_call(
        _ada_body,
        grid=(DEPTH, 6 * D_MODEL // cb),
        in_specs=[
            pl.BlockSpec((n, D_MODEL), lambda l, j: (0, 0)),
            pl.BlockSpec((None, D_MODEL, cb), lambda l, j: (l, 0, j)),
            pl.BlockSpec((None, 1, cb), lambda l, j: (l, 0, j)),
        ],
        out_specs=pl.BlockSpec((None, n, cb), lambda l, j: (l, 0, j)),
        out_shape=jax.ShapeDtypeStruct((DEPTH, n, 6 * D_MODEL), F32),
        compiler_params=_cparams(("arbitrary", "arbitrary")),
        name="ada",
    )(c_all, w_ada, b_ada.reshape(DEPTH, 1, 6 * D_MODEL))


class _Rows:
    def __init__(self, nb, nt, tn, per_row_mod):
        self.nb, self.nt, self.tn, self.per_row_mod = nb, nt, tn, per_row_mod

    def spec(self, width, col=0):
        nt = self.nt
        return pl.BlockSpec((self.tn, width), lambda b, t: (b * nt + t, col))

    def mod_spec(self, j):
        if self.per_row_mod:
            nt = self.nt
            return pl.BlockSpec((self.tn, D_MODEL), lambda b, t: (b * nt + t, j))
        return pl.BlockSpec((None, 1, D_MODEL), lambda b, t: (b, 0, j))

    @property
    def grid(self):
        return (self.nb, self.nt)

    @property
    def rows(self):
        return self.nb * self.nt * self.tn


def _const_spec(shape):
    nd = len(shape)
    return pl.BlockSpec(shape, lambda b, t: (0,) * nd)


def _in_body(x_ref, nw_ref, sh_ref, sc_ref, w_ref, z_ref):
    h = _rms(x_ref[...], nw_ref[...])
    h = h * (1.0 + sc_ref[...]) + sh_ref[...]
    z_ref[...] = jnp.dot(h.astype(BF), w_ref[...], preferred_element_type=F32)


def _in_proj(rows, x, mod, nw, w_bf):
    return pl.pallas_call(
        _in_body,
        grid=rows.grid,
        in_specs=[rows.spec(D_MODEL), _const_spec((1, D_MODEL)),
                  rows.mod_spec(0), rows.mod_spec(1), _const_spec((D_MODEL, D_IN))],
        out_specs=rows.spec(D_IN),
        out_shape=jax.ShapeDtypeStruct((rows.rows, D_IN), F32),
        compiler_params=_cparams(("arbitrary", "arbitrary")),
        name="in_proj",
    )(x, nw, mod, mod, w_bf)


def _out_body(x_ref, ya_ref, yb_ref, yc_ref, yd_ref, on_ref, w_ref, g1_ref, nf_ref,
              sh_ref, sc_ref, x1_ref, h2_ref):
    ys = []
    for g, r in enumerate((ya_ref, yb_ref, yc_ref, yd_ref)):
        ys.append(_rms(r[...], on_ref[:, g * W_GROUP:(g + 1) * W_GROUP]).astype(BF))
    o = jnp.dot(jnp.concatenate(ys, axis=-1), w_ref[...], preferred_element_type=F32)
    x1 = x_ref[...] + g1_ref[...] * o
    x1_ref[...] = x1
    h2 = _rms(x1, nf_ref[...])
    h2_ref[...] = h2 * (1.0 + sc_ref[...]) + sh_ref[...]


def _out_proj(rows, x, ys, mod, on, w_bf, nf):
    sds = jax.ShapeDtypeStruct((rows.rows, D_MODEL), F32)
    return pl.pallas_call(
        _out_body,
        grid=rows.grid,
        in_specs=[rows.spec(D_MODEL)] + [rows.spec(W_GROUP)] * 4
        + [_const_spec((1, D_MODEL)), _const_spec((D_MODEL, D_MODEL)), rows.mod_spec(2),
           _const_spec((1, D_MODEL)), rows.mod_spec(3), rows.mod_spec(4)],
        out_specs=[rows.spec(D_MODEL), rows.spec(D_MODEL)],
        out_shape=[sds, sds],
        compiler_params=_cparams(("arbitrary", "arbitrary")),
        name="out_proj",
    )(x, *ys, on, w_bf, mod, nf, mod, mod)


def _lru_gates(xc, wa, wx, ba, bx, lam):
    r = _sigmoid(_mm3(xc, wa) + ba)
    ig = _sigmoid(_mm3(xc, wx) + bx)
    nl = -lam
    sp = jnp.maximum(nl, 0.0) + jnp.log1p(jnp.exp(-jnp.abs(nl)))
    log_a = (-LRU_C) * sp * r
    a = jnp.exp(log_a)
    x2 = 2.0 * log_a
    em1 = jnp.tanh(0.5 * x2) * (jnp.exp(x2) + 1.0)
    mult = jnp.sqrt(jnp.maximum(-em1, LRU_FLOOR))
    return a, mult * (ig * xc)


def _lru_prompt_body(z_ref, cw_ref, cb_ref, wa_ref, wx_ref, ba_ref, bx_ref, lam_ref,
                     y_ref, hfin_ref, tail_ref, xbuf, hcar, *, tb):
    t = pl.program_id(1)

    @pl.when(t == 0)
    def _():
        xbuf[0:SUBLANES, :] = jnp.zeros((SUBLANES, W_GROUP), F32)
        hcar[...] = jnp.zeros_like(hcar)

    xa = z_ref[:, 0:W_GROUP]
    ga = z_ref[:, W_GROUP:2 * W_GROUP]
    xbuf[SUBLANES:SUBLANES + tb, :] = xa
    xc = cb_ref[...] + cw_ref[3:4, :] * xa
    for k in range(CONV_W - 1):
        xc = xc + cw_ref[k:k + 1, :] * xbuf[pl.ds(SUBLANES - (CONV_W - 1) + k, tb), :]
    xbuf[0:SUBLANES, :] = xa[tb - SUBLANES:tb, :]
    tail_ref[...] = xa[tb - SUBLANES:tb, :]

    a, b = _lru_gates(xc, wa_ref[...], wx_ref[...], ba_ref[...], bx_ref[...], lam_ref[...])
    row = lax.broadcasted_iota(jnp.int32, (tb, W_GROUP), 0)
    d = 1
    while d < tb:
        a_s = pltpu.roll(a, d, 0)
        b_s = pltpu.roll(b, d, 0)
        m = row >= d
        b = jnp.where(m, a * b_s + b, b)
        a = jnp.where(m, a * a_s, a)
        d *= 2
    h = b + a * hcar[0:1, :]
    hl = h[tb - 1:tb, :]
    hcar[...] = jnp.broadcast_to(hl, hcar.shape)
    hfin_ref[...] = hl
    y_ref[...] = h * _gelu(ga)


def _lru_prompt(z, nb, t_len, p, tb=512):
    nt = t_len // tb
    w2 = (1, W_GROUP)
    return pl.pallas_call(
        functools.partial(_lru_prompt_body, tb=tb),
        grid=(nb, nt),
        in_specs=[pl.BlockSpec((tb, 2 * W_GROUP), lambda b, t: (b * nt + t, 0)),
                  _const_spec((CONV_W, W_GROUP)), _const_spec(w2),
                  _const_spec((W_GROUP, W_GROUP)), _const_spec((W_GROUP, W_GROUP)),
                  _const_spec(w2), _const_spec(w2), _const_spec(w2)],
        out_specs=[pl.BlockSpec((tb, W_GROUP), lambda b, t: (b * nt + t, 0)),
                   pl.BlockSpec((None, 1, W_GROUP), lambda b, t: (b, 0, 0)),
                   pl.BlockSpec((None, SUBLANES, W_GROUP), lambda b, t: (b, 0, 0))],
        out_shape=[jax.ShapeDtypeStruct((nb * t_len, W_GROUP), F32),
                   jax.ShapeDtypeStruct((nb, 1, W_GROUP), F32),
                   jax.ShapeDtypeStruct((nb, SUBLANES, W_GROUP), F32)],
        scratch_shapes=[pltpu.VMEM((tb + SUBLANES, W_GROUP), F32),
                        pltpu.VMEM((SUBLANES, W_GROUP), F32)],
        compiler_params=_cparams(("arbitrary", "arbitrary")),
        name="lru_prompt",
    )(z, p["conv_w"], p["conv_b"], p["wa_bd"], p["wx_bd"], p["lru_ba"], p["lru_bx"], p["lru_lambda"])


def _sgu_prompt_body(z_ref, gain_ref, ws_ref, bias_ref, y_ref, *, tb):
    u = _gelu(z_ref[:, 0:W_GROUP])
    v = _rms(_gelu(z_ref[:, W_GROUP:2 * W_GROUP]), gain_ref[...])
    masks = [_lane_head_mask(W_GROUP, g) for g in range(W_GROUP // HEAD_DIM)]
    ws = ws_ref[...]
    bias = bias_ref[...]
    for j in range(tb // CHUNK_B):
        vj = v[j * CHUNK_B:(j + 1) * CHUNK_B, :]
        rhs = jnp.concatenate([vj * m for m in masks], axis=0)
        mix = _mm3(ws, rhs) + bias
        y_ref[j * CHUNK_B:(j + 1) * CHUNK_B, :] = u[j * CHUNK_B:(j + 1) * CHUNK_B, :] * mix


def _sgu_prompt(z, nb, t_len, p, tb=512):
    nt = t_len // tb
    nh = W_GROUP // HEAD_DIM
    return pl.pallas_call(
        functools.partial(_sgu_prompt_body, tb=tb),
        grid=(nb, nt),
        in_specs=[pl.BlockSpec((tb, 2 * W_GROUP), lambda b, t: (b * nt + t, 1)),
                  _const_spec((1, W_GROUP)), _const_spec((CHUNK_B, nh * CHUNK_B)),
                  _const_spec((CHUNK_B, W_GROUP))],
        out_specs=pl.BlockSpec((tb, W_GROUP), lambda b, t: (b * nt + t, 0)),
        out_shape=jax.ShapeDtypeStruct((nb * t_len, W_GROUP), F32),
        compiler_params=_cparams(("arbitrary", "arbitrary")),
        name="sgu_prompt",
    )(z, p["sgu_norm"], p["sgu_ws_cat"], p["sgu_bias"])


def _ab_sample_body(z_ref, h0_ref, buf_ref, cw_ref, cb_ref, wa_ref, wx_ref, ba_ref, bx_ref,
                    lam_ref, gain_ref, mw_ref, mb_ref,
                    ya_ref, hfin_ref, nbuf_ref, yb_ref, vrow_ref, *, nb, nt):
    xa = z_ref[:, 0:W_GROUP]
    ga = z_ref[:, W_GROUP:2 * W_GROUP]
    slabs = [buf_ref[k] for k in range(CONV_W - 1)] + [xa[t * nb:(t + 1) * nb, :] for t in range(nt)]
    xcs = []
    for t in range(nt):
        xc = cb_ref[...]
        for k in range(CONV_W):
            xc = xc + cw_ref[k:k + 1, :] * slabs[t + k]
        xcs.append(xc)
    for k in range(CONV_W - 1):
        nbuf_ref[k] = slabs[nt + k]
    xc = jnp.concatenate(xcs, axis=0)
    a, b = _lru_gates(xc, wa_ref[...], wx_ref[...], ba_ref[...], bx_ref[...], lam_ref[...])
    h = h0_ref[...]
    hs = []
    for t in range(nt):
        h = a[t * nb:(t + 1) * nb, :] * h + b[t * nb:(t + 1) * nb, :]
        hs.append(h)
    hfin_ref[...] = h
    ya_ref[...] = jnp.concatenate(hs, axis=0) * _gelu(ga)

    u = _gelu(z_ref[:, 2 * W_GROUP:3 * W_GROUP])
    v = _rms(_gelu(z_ref[:, 3 * W_GROUP:4 * W_GROUP]), gain_ref[...])
    vrow_ref[...] = v
    for t in range(nt):
        mix = mb_ref[t]
        for s in range(t + 1):
            mix = mix + mw_ref[t, s] * v[s * nb:(s + 1) * nb, :]
        yb_ref[t * nb:(t + 1) * nb, :] = u[t * nb:(t + 1) * nb, :] * mix


def _ab_sample(z, h0, buf, p, nb, nt):
    n = nb * nt
    full = lambda shape: pl.BlockSpec(shape, lambda i: (0,) * len(shape))
    w2 = (1, W_GROUP)
    sd = lambda *s: jax.ShapeDtypeStruct(s, F32)
    return pl.pallas_call(
        functools.partial(_ab_sample_body, nb=nb, nt=nt),
        grid=(1,),
        in_specs=[full((n, 4 * W_GROUP)), full((nb, W_GROUP)), full((CONV_W - 1, nb, W_GROUP)),
                  full((CONV_W, W_GROUP)), full(w2), full((W_GROUP, W_GROUP)),
                  full((W_GROUP, W_GROUP)), full(w2), full(w2), full(w2), full(w2),
                  full((nt, nt, 1, W_GROUP)), full((nt, 1, W_GROUP))],
        out_specs=[full((n, W_GROUP)), full((nb, W_GROUP)), full((CONV_W - 1, nb, W_GROUP)),
                   full((n, W_GROUP)), full((n, W_GROUP))],
        out_shape=[sd(n, W_GROUP), sd(nb, W_GROUP), sd(CONV_W - 1, nb, W_GROUP),
                   sd(n, W_GROUP), sd(n, W_GROUP)],
        compiler_params=_cparams(("arbitrary",)),
        name="ab_sample",
    )(z, h0, buf, p["conv_w"], p["conv_b"], p["wa_bd"], p["wx_bd"], p["lru_ba"], p["lru_bx"],
      p["lru_lambda"], p["sgu_norm"], p["sgu_mw_s"], p["sgu_mb_s"])


def _swa_prompt_body(sink_ref, z_ref, cos_ref, sin_ref, qg_ref, kg_ref,
                     y_ref, klast_ref, vlast_ref, kprev, vprev, *, tb):
    t = pl.program_id(1)

    @pl.when(t == 0)
    def _():
        kprev[...] = jnp.zeros_like(kprev)
        vprev[...] = jnp.zeros_like(vprev)

    cos = cos_ref[...]
    sin = sin_ref[...]
    q = _head_rms(z_ref[:, 0:2 * LANES], qg_ref[...])
    q = _rope(q, jnp.concatenate([cos, cos], axis=-1), jnp.concatenate([sin, sin], axis=-1))
    k = _rope(_head_rms(z_ref[:, 2 * LANES:3 * LANES], kg_ref[...]), cos, sin)
    v = z_ref[:, 3 * LANES:4 * LANES]
    masks = [_lane_head_mask(LANES, j) for j in range(N_KV_C)]
    r = lax.broadcasted_iota(jnp.int32, (WINDOW, 2 * WINDOW), 0)
    c = lax.broadcasted_iota(jnp.int32, (WINDOW, 2 * WINDOW), 1)
    diff = r + WINDOW - c
    band = (diff >= 0) & (diff < WINDOW)
    nsub = tb // WINDOW
    for i in range(nsub):
        sl = slice(i * WINDOW, (i + 1) * WINDOW)
        ki, vi = k[sl, :], v[sl, :]
        kk = jnp.concatenate([kprev[...], ki], axis=0)
        vv = jnp.concatenate([vprev[...], vi], axis=0)
        first = (t * nsub + i) == 0
        ok = band & ((c >= WINDOW) | jnp.logical_not(first))
        outs = []
        for g in range(N_Q_C // N_KV_C):
            qg = q[sl, g * LANES:(g + 1) * LANES]
            og = jnp.zeros((WINDOW, LANES), F32)
            for j in range(N_KV_C):
                s = _mm3(qg * masks[j], kk, _NT) * (HEAD_DIM ** -0.5)
                s = jnp.where(ok, s, NEG_BIG)
                sink = sink_ref[j * (N_Q_C // N_KV_C) + g]
                m = jnp.maximum(jnp.max(s, axis=-1, keepdims=True), sink)
                e = jnp.exp(s - m)
                den = jnp.sum(e, axis=-1, keepdims=True) + jnp.exp(sink - m)
                og = og + _mm3(e / den, vv * masks[j])
            outs.append(og)
        y_ref[sl, :] = jnp.concatenate(outs, axis=-1)
        kprev[...] = ki
        vprev[...] = vi
    klast_ref[...] = k[tb - WINDOW:tb, :]
    vlast_ref[...] = v[tb - WINDOW:tb, :]


def _swa_prompt(z, nb, t_len, p, cos_t, sin_t, tb=512):
    nt = t_len // tb
    sd = lambda *s: jax.ShapeDtypeStruct(s, F32)
    return pl.pallas_call(
        functools.partial(_swa_prompt_body, tb=tb),
        grid=(nb, nt),
        in_specs=[pl.BlockSpec(memory_space=pltpu.SMEM),
                  pl.BlockSpec((tb, 4 * LANES), lambda b, t: (b * nt + t, 2)),
                  pl.BlockSpec((tb, LANES), lambda b, t: (t, 0)),
                  pl.BlockSpec((tb, LANES), lambda b, t: (t, 0)),
                  _const_spec((1, 2 * LANES)), _const_spec((1, LANES))],
        out_specs=[pl.BlockSpec((tb, 2 * LANES), lambda b, t: (b * nt + t, 0)),
                   pl.BlockSpec((None, WINDOW, LANES), lambda b, t: (b, 0, 0)),
                   pl.BlockSpec((None, WINDOW, LANES), lambda b, t: (b, 0, 0))],
        out_shape=[sd(nb * t_len, 2 * LANES), sd(nb, WINDOW, LANES), sd(nb, WINDOW, LANES)],
        scratch_shapes=[pltpu.VMEM((WINDOW, LANES), F32), pltpu.VMEM((WINDOW, LANES), F32)],
        compiler_params=_cparams(("arbitrary", "arbitrary")),
        name="swa_prompt",
    )(p["sinks"], z, cos_t, sin_t, p["q_gain"], p["k_gain"])


def _swa_sample_body(z_ref, cos_ref, sin_ref, qg_ref, kg_ref, sink_ref, kb_ref, vb_ref,
                     y_ref, knew_ref, *, bb, nt):
    seg = _seg_ones(LANES, HEAD_DIM)
    kb = kb_ref[...]
    vb = vb_ref[...]
    nkeys = kb.shape[0]
    kidx = lax.broadcasted_iota(jnp.int32, kb.shape, 0)
    qs, ks, vs = [], [], []
    for t in range(nt):
        zt = z_ref[t]
        cos, sin = cos_ref[t], sin_ref[t]
        q = _head_rms(zt[:, 0:2 * LANES], qg_ref[...])
        qs.append(_rope(q, jnp.concatenate([cos, cos], axis=-1), jnp.concatenate([sin, sin], axis=-1)))
        kt = _rope(_head_rms(zt[:, 2 * LANES:3 * LANES], kg_ref[...]), cos, sin)
        ks.append(kt)
        vs.append(zt[:, 3 * LANES:4 * LANES])
        knew_ref[t] = kt
    scale = HEAD_DIM ** -0.5
    for t in range(nt):
        outs = []
        for g in range(N_Q_C // N_KV_C):
            qtg = qs[t][:, g * LANES:(g + 1) * LANES]
            prod = (qtg[None, :, :] * kb).reshape(nkeys * bb, LANES)
            sb = (_mm_x01(prod, seg) * scale).reshape(nkeys, bb, LANES)
            sb = jnp.where(kidx >= t + 1 + (nkeys - WINDOW), sb, NEG_BIG)
            sn = [_mm_x01(qtg * ks[s], seg) * scale for s in range(t + 1)]
            sink = sink_ref[g]
            m = jnp.maximum(jnp.max(sb, axis=0), sink)
            for x in sn:
                m = jnp.maximum(m, x)
            eb = jnp.exp(sb - m[None, :, :])
            den = jnp.sum(eb, axis=0) + jnp.exp(sink - m)
            num = jnp.sum(eb * vb, axis=0)
            for s, x in enumerate(sn):
                en = jnp.exp(x - m)
                den = den + en
                num = num + en * vs[s]
            outs.append(num / den)
        y_ref[t] = jnp.concatenate(outs, axis=-1)


def _swa_sample(z3, cos_s, sin_s, p, kb_t, vb_t, nb, nt, bb=16):
    sd = lambda *s: jax.ShapeDtypeStruct(s, F32)
    nk = kb_t.shape[0]
    c3 = lambda shape: pl.BlockSpec(shape, lambda i: (0,) * len(shape))
    return pl.pallas_call(
        functools.partial(_swa_sample_body, bb=bb, nt=nt),
        grid=(nb // bb,),
        in_specs=[pl.BlockSpec((nt, bb, 4 * LANES), lambda i: (0, i, 2)),
                  c3((nt, 1, LANES)), c3((nt, 1, LANES)), c3((1, 2 * LANES)), c3((1, LANES)),
                  c3((N_Q_C // N_KV_C, 1, LANES)),
                  pl.BlockSpec((nk, bb, LANES), lambda i: (0, i, 0)),
                  pl.BlockSpec((nk, bb, LANES), lambda i: (0, i, 0))],
        out_specs=[pl.BlockSpec((nt, bb, 2 * LANES), lambda i: (0, i, 0)),
                   pl.BlockSpec((nt, bb, LANES), lambda i: (0, i, 0))],
        out_shape=[sd(nt, nb, 2 * LANES), sd(nt, nb, LANES)],
        compiler_params=_cparams(("arbitrary",)),
        name="swa_sample",
    )(z3, cos_s, sin_s, p["q_gain"], p["k_gain"], p["sink_lanes"], kb_t, vb_t)


def _hgrn_body(*refs, rows, has_init, n_valid):
    if has_init:
        (zqf_ref, zig_ref, lb_ref, gn_ref, s0_ref, y_ref, sfin_ref,
         st, qq_s, qe_s, kk_s, kd_s, bc_s, bl_s, v_s, o_s) = refs
    else:
        (zqf_ref, zig_ref, lb_ref, gn_ref, y_ref, sfin_ref,
         st, qq_s, qe_s, kk_s, kd_s, bc_s, bl_s, v_s, o_s) = refs
        s0_ref = None
    t = pl.program_id(1)

    @pl.when(t == 0)
    def _():
        if has_init:
            st[...] = s0_ref[...]
        else:
            st[...] = jnp.zeros_like(st)

    sup = HG_SUPER
    n_super = max(rows // sup, 1)
    real = min(rows, sup)
    lb = lb_ref[...]
    seg_heads = _seg_ones(W_GROUP, HEAD_DIM)
    bd_mask = seg_heads.astype(F32)
    ri = lax.broadcasted_iota(jnp.int32, (sup, sup), 0)
    ci = lax.broadcasted_iota(jnp.int32, (sup, sup), 1)
    same = (ri // HG_SUB) == (ci // HG_SUB)
    tri_bd = jnp.where(same & (ci <= ri), 1.0, 0.0).astype(BF)
    ones_bd = jnp.where(same, 1.0, 0.0).astype(BF)
    row_sup = lax.broadcasted_iota(jnp.int32, (sup, W_GROUP), 0)
    pr = lax.broadcasted_iota(jnp.int32, (HG_SUB * HG_SUB, W_GROUP), 0)
    causal = jnp.where((pr % HG_SUB) >= (pr // HG_SUB), 1.0, 0.0)

    def pad(x):
        if real == sup:
            return x
        return jnp.concatenate([x, jnp.zeros((sup - real, x.shape[1]), F32)], axis=0)

    for sb in range(n_super):
        rs = slice(sb * sup, sb * sup + real)
        zf = zqf_ref[rs, W_GROUP:2 * W_GROUP]
        f = lb + (1.0 - lb) * _sigmoid(zf)
        logf = jnp.log(jnp.maximum(f, 1e-30))
        kk = (1.0 - lb) * _sigmoid(-zf)
        if n_valid < real:
            valid = lax.broadcasted_iota(jnp.int32, (real, W_GROUP), 0) < n_valid
            logf = jnp.where(valid, logf, 0.0)
            kk = jnp.where(valid, kk, 0.0)
        logf, kk = pad(logf), pad(kk)
        qq = pad(_silu(zqf_ref[rs, 0:W_GROUP]))
        v = pad(zig_ref[rs, 0:W_GROUP])
        bc = _mm_01x(tri_bd, logf)
        bl = _mm_01x(ones_bd, logf)
        qq_s[...] = qq
        kk_s[...] = kk
        bc_s[...] = bc
        bl_s[...] = bl
        qe_s[...] = qq * jnp.exp(bc)
        kd = kk * jnp.exp(bl - bc)
        v_s[...] = v
        vt = v.T

        n_sub = -(-real // HG_SUB)

        def sub(c, carry):
            r0 = pl.multiple_of(c * HG_SUB, HG_SUB)
            rsub = pl.ds(r0, HG_SUB)
            bc_c = bc_s[rsub, :]
            qq_c = qq_s[rsub, :]
            kk_c = kk_s[rsub, :]
            v_c = v_s[rsub, :]
            parts = []
            for s in range(HG_SUB):
                e = jnp.exp(jnp.minimum(bc_c - bc_c[s:s + 1, :], 0.0))
                parts.append(qq_c * e * kk_c[s:s + 1, :])
            pm = jnp.concatenate(parts, axis=0) * causal
            att = _mm_x01(pm, seg_heads)
            o_c = _mm3(qe_s[rsub, :], st[...], _NT)
            for s in range(HG_SUB):
                o_c = o_c + att[s * HG_SUB:(s + 1) * HG_SUB, :] * v_c[s:s + 1, :]
            o_s[pl.ds(pl.multiple_of(sb * sup + r0, HG_SUB), HG_SUB), :] = o_c
            kd_c = jnp.where((row_sup // HG_SUB) == c, kd, 0.0)
            decay = jnp.exp(bl_s[pl.ds(r0, 1), :])
            st[...] = st[...] * decay + _mm3(vt, kd_c) * bd_mask
            return carry

        lax.fori_loop(0, n_sub, sub, 0)

    o = o_s[0:rows, :]
    zg = zig_ref[:, W_GROUP:2 * W_GROUP]
    y_ref[...] = _head_rms(o, gn_ref[...]) * _silu(zg)
    sfin_ref[...] = st[...]


def _hgrn(zqf_src, zig_src, qf_col, ig_col, nseq, t_len, rows, p, s0=None, n_valid=None):
    nt = t_len // rows
    n_valid = rows if n_valid is None else n_valid
    sd = lambda *s: jax.ShapeDtypeStruct(s, F32)
    in_specs = [pl.BlockSpec((rows, 2 * W_GROUP), lambda b, t: (b * nt + t, qf_col)),
                pl.BlockSpec((rows, 2 * W_GROUP), lambda b, t: (b * nt + t, ig_col)),
                _const_spec((1, W_GROUP)), _const_spec((1, W_GROUP))]
    args = [zqf_src, zig_src, p["hgrn_lb"], p["hgrn_gn"]]
    if s0 is not None:
        in_specs.append(pl.BlockSpec((None, W_GROUP, W_GROUP), lambda b, t: (b, 0, 0)))
        args.append(s0)
    scr_rows = max(rows, HG_SUPER)
    sup_shape = pltpu.VMEM((HG_SUPER, W_GROUP), F32)
    return pl.pallas_call(
        functools.partial(_hgrn_body, rows=rows, has_init=s0 is not None, n_valid=n_valid),
        grid=(nseq, nt),
        in_specs=in_specs,
        out_specs=[pl.BlockSpec((rows, W_GROUP), lambda b, t: (b * nt + t, 0)),
                   pl.BlockSpec((None, W_GROUP, W_GROUP), lambda b, t: (b, 0, 0))],
        out_shape=[sd(nseq * t_len, W_GROUP), sd(nseq, W_GROUP, W_GROUP)],
        scratch_shapes=[pltpu.VMEM((W_GROUP, W_GROUP), F32)] + [sup_shape] * 7
        + [pltpu.VMEM((scr_rows, W_GROUP), F32)],
        compiler_params=_cparams(("arbitrary", "arbitrary")),
        name="hgrn",
    )(*args)


def _oddeven_merge_sort(n):
    pairs = []

    def merge(lo, m, r):
        step = r * 2
        if step < m:
            merge(lo, m, step)
            merge(lo + r, m, step)
            for i in range(lo + r, lo + m - r, step):
                pairs.append((i, i + r))
        else:
            pairs.append((lo, lo + r))

    def sort(lo, m):
        if m > 1:
            h = m // 2
            sort(lo, h)
            sort(lo + h, h)
            merge(lo, m, 1)

    sort(0, n)
    return pairs


_NET16 = _oddeven_merge_sort(PEER_TOPK)
_CELLS = [(a, b) for a in range(PEER_TOPK) for b in range(PEER_TOPK)
          if (a + 1) * (b + 1) <= PEER_TOPK]


def _beats(va, ia, vb, ib):
    return (va > vb) | ((va == vb) & (ia < ib))


def _cmpx(items, i, j):
    (va, ia), (vb, ib) = items[i], items[j]
    gt = _beats(va, ia, vb, ib)
    items[i] = (jnp.where(gt, va, vb), jnp.where(gt, ia, ib))
    items[j] = (jnp.where(gt, vb, va), jnp.where(gt, ib, ia))


def _merge_top(a_items, b_items):
    n = len(a_items)
    items = []
    for i in range(n):
        (va, ia), (vb, ib) = a_items[i], b_items[n - 1 - i]
        gt = _beats(va, ia, vb, ib)
        items.append((jnp.where(gt, va, vb), jnp.where(gt, ia, ib)))
    d = n // 2
    while d >= 1:
        for i in range(n):
            if not (i & d):
                _cmpx(items, i, i + d)
        d //= 2
    return items


def _top16_sorted(load_key, n_keys, shape):
    stack = []
    for g in range(n_keys // PEER_TOPK):
        items = [(load_key(g * PEER_TOPK + j), jnp.full(shape, float(g * PEER_TOPK + j), F32))
                 for j in range(PEER_TOPK)]
        for i, j in _NET16:
            _cmpx(items, i, j)
        level = 0
        while stack and stack[-1][0] == level:
            _, other = stack.pop()
            items = _merge_top(other, items)
            level += 1
        stack.append((level, items))
    assert len(stack) == 1
    return stack[0][1]


def _route_body(h_ref, wqh_ref, wql_ref, kbh_ref, kbl_ref, e1_ref, e2_ref, g_ref,
                st_s, top_s, *, tn):
    hh, hl = _hi_lo(h_ref[...])
    wqh, wql = wqh_ref[...], wql_ref[...]
    qt = (_dg(wqh, hl, _NT) + _dg(wql, hh, _NT)) + _dg(wqh, hh, _NT)
    half = PEER_HEADS * PEER_DQ // 2
    for p in range(2):
        qp = qt[p * half:(p + 1) * half, :]
        qh, ql = _hi_lo(qp)
        kh, kl = kbh_ref[p], kbl_ref[p]
        st_s[p] = (_dg(kh, ql, _NN) + _dg(kl, qh, _NN)) + _dg(kh, qh, _NN)

    shape = (PEER_HEADS, LANES)

    def column(col, carry):
        c0 = pl.multiple_of(col * LANES, LANES)
        cs = pl.ds(c0, LANES)
        for p in range(2):
            top = _top16_sorted(lambda k: st_s[p, k * PEER_HEADS:(k + 1) * PEER_HEADS, cs],
                                PEER_KEYS, shape)
            for a, (va, ia) in enumerate(top):
                top_s[p, 0, a] = va
                top_s[p, 1, a] = ia
        v1 = [top_s[0, 0, a] for a in range(PEER_TOPK)]
        v2 = [top_s[1, 0, b] for b in range(PEER_TOPK)]
        cval = {cell: v1[cell[0]] + v2[cell[1]] for cell in _CELLS}
        rank_static = {cell: float((cell[0] + 1) * (cell[1] + 1) - 1) for cell in _CELLS}
        dyn = {cell: None for cell in _CELLS}
        for xi, x in enumerate(_CELLS):
            for y in _CELLS[xi + 1:]:
                if x[0] < y[0] and x[1] > y[1]:
                    xw = jnp.where(cval[x] >= cval[y], 1.0, 0.0)
                    dyn[y] = xw if dyn[y] is None else dyn[y] + xw
                    lose = 1.0 - xw
                    dyn[x] = lose if dyn[x] is None else dyn[x] + lose
        rank = {cell: (rank_static[cell] if dyn[cell] is None else dyn[cell] + rank_static[cell])
                for cell in _CELLS}
        c00 = cval[(0, 0)]
        ex = {}
        zsum = None
        for cell in _CELLS:
            if dyn[cell] is None:
                w = jnp.exp(cval[cell] - c00)
            else:
                w = jnp.where(rank[cell] < PEER_TOPK, jnp.exp(cval[cell] - c00), 0.0)
            ex[cell] = w
            zsum = w if zsum is None else zsum + w
        inv = 1.0 / zsum
        for k in range(PEER_TOPK):
            e1 = jnp.zeros(shape, F32)
            e2 = jnp.zeros(shape, F32)
            gk = jnp.zeros(shape, F32)
            for cell in _CELLS:
                lo = (cell[0] + 1) * (cell[1] + 1) - 1
                if lo > k:
                    continue
                if dyn[cell] is None:
                    if lo != k:
                        continue
                    e1, e2, gk = top_s[0, 1, cell[0]], top_s[1, 1, cell[1]], ex[cell]
                    continue
                hit = rank[cell] == float(k)
                e1 = jnp.where(hit, top_s[0, 1, cell[0]], e1)
                e2 = jnp.where(hit, top_s[1, 1, cell[1]], e2)
                gk = jnp.where(hit, ex[cell], gk)
            rows = slice(k * PEER_HEADS, (k + 1) * PEER_HEADS)
            e1_ref[rows, cs] = e1
            e2_ref[rows, cs] = e2
            g_ref[rows, cs] = gk * inv
        return carry

    lax.fori_loop(0, tn // LANES, column, 0)


def _route(h2, p, tn=256):
    n = h2.shape[0]
    nslots = PEER_TOPK * PEER_HEADS
    half = PEER_HEADS * PEER_DQ // 2
    nk = PEER_KEYS * PEER_HEADS
    sd = jax.ShapeDtypeStruct((nslots, n), F32)
    c1 = lambda shape: pl.BlockSpec(shape, lambda i: (0,) * len(shape))
    return pl.pallas_call(
        functools.partial(_route_body, tn=tn),
        grid=(n // tn,),
        in_specs=[pl.BlockSpec((tn, D_MODEL), lambda i: (i, 0)),
                  c1((2 * half, D_MODEL)), c1((2 * half, D_MODEL)),
                  c1((2, nk, half)), c1((2, nk, half))],
        out_specs=[pl.BlockSpec((nslots, tn), lambda i: (0, i))] * 3,
        out_shape=[sd, sd, sd],
        scratch_shapes=[pltpu.VMEM((2, nk, tn), F32),
                        pltpu.VMEM((2, 2, PEER_TOPK, PEER_HEADS, LANES), F32)],
        compiler_params=_cparams(("arbitrary",)),
        name="peer_route",
    )(h2, p["wq_t_hi"], p["wq_t_lo"], p["kbig_hi"], p["kbig_lo"])


PEER_E1_STEP = 8
PEER_STEP = PEER_E1_STEP * PEER_KEYS
PEER_BUILD_UNROLL = 2 * SUBLANES
PEER_PAIRS = PEER_KEYS // 2
HI16 = 0xFFFF0000


def _peer_body(h_ref, e1_ref, e2_ref, g_ref, u_ref, v_ref, x1_ref, g2_ref, o_ref,
               gt_s, xb_s, acc_s, *, tn):
    cp = pl.program_id(2)

    @pl.when(cp == 0)
    def _():
        xb_s[...] = h_ref[...].astype(BF)
        acc_s[...] = jnp.zeros_like(acc_s)
        r = lax.broadcasted_iota(jnp.int32, (PEER_KEYS, LANES), 0)
        sub1 = jnp.where(r < PEER_PAIRS, 2 * r, 2 * (r - PEER_PAIRS) + 1).astype(F32)
        sub2 = r.astype(F32)

        def build(nb, carry):
            r0 = pl.multiple_of(nb * PEER_BUILD_UNROLL, PEER_BUILD_UNROLL)
            e1s = e1_ref[pl.ds(r0, PEER_BUILD_UNROLL), :]
            e2s = e2_ref[pl.ds(r0, PEER_BUILD_UNROLL), :]
            gs = g_ref[pl.ds(r0, PEER_BUILD_UNROLL), :]
            for j in range(PEER_BUILD_UNROLL):
                pt = jnp.where(sub1 == e1s[j:j + 1, :], 1.0, 0.0).astype(BF)
                qg = jnp.where(sub2 == e2s[j:j + 1, :], gs[j:j + 1, :], 0.0).astype(BF)
                gn = _dg(pt, qg, _NT).astype(BF).astype(F32)
                even = lax.bitcast_convert_type(gn[0:PEER_PAIRS, :], jnp.uint32)
                odd = lax.bitcast_convert_type(gn[PEER_PAIRS:PEER_KEYS, :], jnp.uint32)
                row = pl.multiple_of((r0 + j) * PEER_PAIRS, PEER_PAIRS)
                gt_s[pl.ds(row, PEER_PAIRS), :] = (odd & jnp.uint32(HI16)) | (even >> 16)
            return carry

        lax.fori_loop(0, tn // PEER_BUILD_UNROLL, build, 0)

    a = _gelu(_dg(xb_s[...], u_ref[...], _NT))
    gates = []
    for j in range(PEER_E1_STEP // 2):
        w = gt_s[pl.ds((PEER_E1_STEP // 2) * cp + j, tn, stride=PEER_PAIRS), :]
        gates.append(lax.bitcast_convert_type(w << 16, F32))
        gates.append(lax.bitcast_convert_type(w & jnp.uint32(HI16), F32))
    ga = (a * jnp.concatenate(gates, axis=-1)).astype(BF)
    acc_s[...] += jnp.dot(ga, v_ref[...], preferred_element_type=F32)

    @pl.when(cp == pl.num_programs(2) - 1)
    def _():
        o_ref[...] = x1_ref[...] + g2_ref[...] * acc_s[...]


def _peer(rows, h2, e1, e2, g, u_bf, v_bf, x1, mod):
    tn = rows.tn
    nt = rows.nt
    ncp = PEER_KEYS * PEER_KEYS // PEER_STEP
    tok = lambda width: pl.BlockSpec((tn, width), lambda b, t, c: (b * nt + t, 0))
    if rows.per_row_mod:
        g2_spec = pl.BlockSpec((tn, D_MODEL), lambda b, t, c: (b * nt + t, 5))
    else:
        g2_spec = pl.BlockSpec((None, 1, D_MODEL), lambda b, t, c: (b, 0, 5))
    return pl.pallas_call(
        functools.partial(_peer_body, tn=tn),
        grid=(rows.nb, rows.nt, ncp),
        in_specs=[tok(D_MODEL), tok(LANES), tok(LANES), tok(LANES),
                  pl.BlockSpec((PEER_STEP, D_MODEL), lambda b, t, c: (c, 0)),
                  pl.BlockSpec((PEER_STEP, D_MODEL), lambda b, t, c: (c, 0)),
                  tok(D_MODEL), g2_spec],
        out_specs=tok(D_MODEL),
        out_shape=jax.ShapeDtypeStruct((rows.rows, D_MODEL), F32),
        scratch_shapes=[pltpu.VMEM((tn * PEER_PAIRS, LANES), jnp.uint32),
                        pltpu.VMEM((tn, D_MODEL), BF),
                        pltpu.VMEM((tn, D_MODEL), F32)],
        compiler_params=_cparams(("arbitrary", "arbitrary", "arbitrary")),
        name="peer_experts",
    )(h2, e1, e2, g, u_bf, v_bf, x1, mod)


_Q_HEAD_ORDER = (0, 2, 1, 3)


def _q_perm():
    return np.concatenate([np.arange(h * HEAD_DIM, (h + 1) * HEAD_DIM) for h in _Q_HEAD_ORDER])


def _block_diag(w):
    nblk, bi, bo = w.shape
    out = jnp.zeros((nblk * bi, nblk * bo), w.dtype)
    for h in range(nblk):
        out = out.at[h * bi:(h + 1) * bi, h * bo:(h + 1) * bo].set(w[h])
    return out


def _rope_tables(pos):
    half = HEAD_DIM // 2
    freqs = ROPE_THETA ** (-jnp.arange(half, dtype=F32) / half)
    ang = pos.astype(F32)[:, None] * freqs[None, :]
    cos, sin = jnp.cos(ang), jnp.sin(ang)
    cos_h = jnp.concatenate([cos, cos], axis=-1)
    sin_h = jnp.concatenate([-sin, sin], axis=-1)
    reps = LANES // HEAD_DIM
    return jnp.tile(cos_h, (1, reps)), jnp.tile(sin_h, (1, reps))


def _layer_params(i, nt_s, lower_bound, w_in, conv_w, conv_b, lru_wa, lru_ba, lru_wx, lru_bx,
                  lru_lambda, sgu_norm, sgu_ws, sgu_b, q_norm, k_norm, sinks, hgrn_gnorm,
                  out_norm, w_out, norm_ffn, norm_mix, peer_wq, peer_keys, peer_u, peer_v):
    row = lambda v: v.reshape(1, -1)
    qperm = _q_perm()
    q0 = 4 * W_GROUP
    in_perm = np.arange(D_IN)
    in_perm[q0:q0 + W_GROUP] = q0 + qperm
    c0 = 2 * W_GROUP
    mix_perm = np.arange(N_GROUPS * W_GROUP)
    mix_perm[c0:c0 + W_GROUP] = c0 + qperm
    nh = W_GROUP // HEAD_DIM
    causal = jnp.tril(jnp.ones((CHUNK_B, CHUNK_B), F32))
    ws_c = sgu_ws[i] * causal
    kbig = jnp.zeros((2, PEER_KEYS, PEER_HEADS, PEER_HEADS, PEER_DQ // 2), F32)
    for h in range(PEER_HEADS):
        kbig = kbig.at[:, :, h, h, :].set(peer_keys[i, h])
    kbig = kbig.reshape(2, PEER_KEYS * PEER_HEADS, PEER_HEADS * PEER_DQ // 2)
    wq_t = peer_wq[i].reshape(D_MODEL, PEER_HEADS, 2, PEER_DQ // 2)
    wq_t = wq_t.transpose(2, 1, 3, 0).reshape(PEER_HEADS * PEER_DQ, D_MODEL)
    wq_hi, wq_lo = _hi_lo(wq_t)
    kb_hi, kb_lo = _hi_lo(kbig)
    sink_q = sinks[i]
    sink_lanes = jnp.stack([
        jnp.concatenate([jnp.full((HEAD_DIM,), sink_q[j * 2 + g]) for j in range(N_KV_C)])
        for g in range(N_Q_C // N_KV_C)]).reshape(N_Q_C // N_KV_C, 1, LANES)
    return {
        "norm_mix": row(norm_mix[i]),
        "w_in": w_in[i][:, in_perm].astype(BF),
        "conv_w": conv_w[i], "conv_b": row(conv_b[i]),
        "wa_bd": _block_diag(lru_wa[i]), "wx_bd": _block_diag(lru_wx[i]),
        "lru_ba": row(lru_ba[i]), "lru_bx": row(lru_bx[i]), "lru_lambda": row(lru_lambda[i]),
        "sgu_norm": row(sgu_norm[i]),
        "sgu_ws_cat": jnp.concatenate([ws_c[g] for g in range(nh)], axis=1),
        "sgu_bias": jnp.repeat(sgu_b[i].T, HEAD_DIM, axis=1),
        "sgu_mw_s": jnp.repeat(ws_c[:, :nt_s, :nt_s].transpose(1, 2, 0), HEAD_DIM,
                               axis=-1).reshape(nt_s, nt_s, 1, W_GROUP),
        "sgu_mb_s": jnp.repeat(sgu_b[i][:, :nt_s].T, HEAD_DIM, axis=-1).reshape(nt_s, 1, W_GROUP),
        "q_gain": row(jnp.tile(q_norm[i], N_Q_C)), "k_gain": row(jnp.tile(k_norm[i], N_KV_C)),
        "sinks": sink_q, "sink_lanes": sink_lanes,
        "hgrn_lb": row(lower_bound[i]), "hgrn_gn": row(jnp.tile(hgrn_gnorm[i], HGRN_HEADS)),
        "out_norm": row(out_norm[i][mix_perm]),
        "w_out": w_out[i][mix_perm, :].astype(BF),
        "norm_ffn": row(norm_ffn[i]),
        "wq_t_hi": wq_hi, "wq_t_lo": wq_lo, "kbig_hi": kb_hi, "kbig_lo": kb_lo,
        "peer_u": peer_u[i].astype(BF), "peer_v": peer_v[i].astype(BF),
    }


def _state_to_bd_t(s):
    b = s.shape[0]
    st = jnp.zeros((b, HGRN_HEADS, HEAD_DIM, HGRN_HEADS, HEAD_DIM), F32)
    for h in range(HGRN_HEADS):
        st = st.at[:, h, :, h, :].set(jnp.swapaxes(s[:, h], 1, 2))
    return st.reshape(b, W_GROUP, W_GROUP)


def _bd_t_to_state(st):
    b = st.shape[0]
    s5 = st.reshape(b, HGRN_HEADS, HEAD_DIM, HGRN_HEADS, HEAD_DIM)
    return jnp.stack([jnp.swapaxes(s5[:, h, :, h, :], 1, 2) for h in range(HGRN_HEADS)], axis=1)


def _peer_block(rows, h2, x1, mod, p):
    e1t, e2t, gt = _route(h2, p)
    return _peer(rows, h2, e1t.T, e2t.T, gt.T, p["peer_u"], p["peer_v"], x1, mod)


def kernel(x_prompt, x_sample, state_lru_h, state_lru_conv, state_swa_k, state_swa_v, state_hgrn_S,
           c_prompt, c_sample, w_ada, b_ada, norm_mix, w_in, conv_w, conv_b, lru_wa, lru_ba, lru_wx,
           lru_bx, lru_lambda, sgu_norm, sgu_ws, sgu_b, q_norm, k_norm, sinks, hgrn_lb, hgrn_gnorm,
           out_norm, w_out, norm_ffn, peer_wq, peer_keys, peer_u, peer_v):
    bp, tp, _ = x_prompt.shape
    bs, ts, _ = x_sample.shape
    n_s = bs * ts
    win = state_swa_k.shape[2]

    lbp = jax.nn.softmax(hgrn_lb.astype(F32), axis=0)
    lower_bound = jnp.cumsum(lbp, axis=0) - lbp[0]

    c_all = jnp.concatenate([c_prompt, c_sample], axis=0)
    pad_c = (-c_all.shape[0]) % SUBLANES
    mods = _ada(jnp.pad(c_all, ((0, pad_c), (0, 0))), w_ada, b_ada)

    cos_p, sin_p = _rope_tables(jnp.arange(tp, dtype=jnp.int32))
    cos_s, sin_s = _rope_tables(PAST_LEN + jnp.arange(ts, dtype=jnp.int32))
    cos_s, sin_s = cos_s.reshape(ts, 1, LANES), sin_s.reshape(ts, 1, LANES)

    rows_p = _Rows(bp, tp // 512, 512, per_row_mod=False)
    rows_s = _Rows(1, 1, n_s, per_row_mod=True)
    peer_rows_p = _Rows(bp, tp // 512, 512, per_row_mod=False)
    peer_rows_s = _Rows(1, n_s // 512, 512, per_row_mod=True)

    xp = x_prompt.reshape(bp * tp, D_MODEL)
    xs = jnp.swapaxes(x_sample, 0, 1).reshape(n_s, D_MODEL)
    st_p, st_s = [], []
    for i in range(DEPTH):
        p = _layer_params(i, ts, lower_bound, w_in, conv_w, conv_b, lru_wa, lru_ba, lru_wx, lru_bx,
                          lru_lambda, sgu_norm, sgu_ws, sgu_b, q_norm, k_norm, sinks, hgrn_gnorm,
                          out_norm, w_out, norm_ffn, norm_mix, peer_wq, peer_keys, peer_u, peer_v)
        mod_p = mods[i, :bp].reshape(bp, 1, 6 * D_MODEL)
        mod_s = jnp.tile(mods[i, bp:bp + bs], (ts, 1))

        z = _in_proj(rows_p, xp, mod_p, p["norm_mix"], p["w_in"])
        ya, h_fin, tail = _lru_prompt(z, bp, tp, p)
        yb = _sgu_prompt(z, bp, tp, p)
        yc, k_last, v_last = _swa_prompt(z, bp, tp, p, cos_p, sin_p)
        yd, s_fin = _hgrn(z, z, 3, 4, bp, tp, 512, p)
        x1, h2 = _out_proj(rows_p, xp, (ya, yb, yc, yd), mod_p, p["out_norm"], p["w_out"], p["norm_ffn"])
        xp = _peer_block(peer_rows_p, h2, x1, mod_p, p)
        st_p.append((h_fin.reshape(bp, W_GROUP), tail[:, SUBLANES - (CONV_W - 1):, :],
                     k_last.reshape(bp, WINDOW, N_KV_C, HEAD_DIM),
                     v_last.reshape(bp, WINDOW, N_KV_C, HEAD_DIM), _bd_t_to_state(s_fin)))

        zs = _in_proj(rows_s, xs, mod_s, p["norm_mix"], p["w_in"])
        ya, h_fin, nbuf, yb, vrows = _ab_sample(
            zs, state_lru_h[:, i], jnp.swapaxes(state_lru_conv[:, i], 0, 1), p, bs, ts)
        kb = state_swa_k[:, i].reshape(bs, win, LANES)
        vb = state_swa_v[:, i].reshape(bs, win, LANES)
        yc3, knew = _swa_sample(zs.reshape(ts, bs, D_IN), cos_s, sin_s, p,
                                jnp.swapaxes(kb, 0, 1), jnp.swapaxes(vb, 0, 1), bs, ts)
        vnew = zs.reshape(ts, bs, D_IN)[:, :, 4 * W_GROUP + 3 * LANES:4 * W_GROUP + 4 * LANES]
        zd = jnp.swapaxes(zs.reshape(ts, bs, D_IN)[:, :, 6 * W_GROUP:], 0, 1)
        zd = jnp.pad(zd, ((0, 0), (0, HG_SUB - ts), (0, 0))).reshape(bs * HG_SUB, 4 * W_GROUP)
        yd_pad, s_fin = _hgrn(zd, zd, 0, 1, bs, HG_SUB, HG_SUB, p,
                              s0=_state_to_bd_t(state_hgrn_S[:, i]), n_valid=ts)
        yd = jnp.swapaxes(yd_pad.reshape(bs, HG_SUB, W_GROUP)[:, :ts], 0, 1).reshape(n_s, W_GROUP)
        x1, h2 = _out_proj(rows_s, xs, (ya, yb, yc3.reshape(n_s, 2 * LANES), yd), mod_s,
                           p["out_norm"], p["w_out"], p["norm_ffn"])
        xs = _peer_block(peer_rows_s, h2, x1, mod_s, p)
        k_win = jnp.concatenate([kb, jnp.swapaxes(knew, 0, 1)], axis=1)[:, -win:]
        v_win = jnp.concatenate([vb, jnp.swapaxes(vnew, 0, 1)], axis=1)[:, -win:]
        st_s.append((h_fin, jnp.swapaxes(nbuf, 0, 1),
                     k_win.reshape(bs, win, N_KV_C, HEAD_DIM), v_win.reshape(bs, win, N_KV_C, HEAD_DIM),
                     _bd_t_to_state(s_fin),
                     jnp.swapaxes(vrows.reshape(ts, bs, W_GROUP), 0, 1)))

    stack = lambda per_layer, j: jnp.stack([s[j] for s in per_layer], axis=1)
    y_p = xp.reshape(bp, tp, D_MODEL)
    y_s = jnp.swapaxes(xs.reshape(ts, bs, D_MODEL), 0, 1)
    return (y_p, y_s,
            stack(st_p, 0), stack(st_p, 1), stack(st_p, 2), stack(st_p, 3), stack(st_p, 4),
            stack(st_s, 0), stack(st_s, 1), stack(st_s, 2), stack(st_s, 3), stack(st_s, 4),
            stack(st_s, 5))
```

```python
import functools

import numpy as np
import jax
import jax.numpy as jnp
from jax import lax
from jax.experimental import pallas as pl
from jax.experimental.pallas import tpu as pltpu

F32 = jnp.float32
BF = jnp.bfloat16

D_MODEL = 1024
DEPTH = 2
PAST_LEN = 16384
HEAD_DIM = 64
W_GROUP = 256
N_GROUPS = 4
CONV_W = 4
LRU_C = 8.0
LRU_BLOCKS = 4
LRU_FLOOR = 1e-12
CHUNK_B = 128
WINDOW = 128
N_Q_C = 4
N_KV_C = 2
ROPE_THETA = 10000.0
NEG_BIG = -1e30
HGRN_HEADS = 4
PEER_HEADS = 8
PEER_KEYS = 128
PEER_DQ = 128
PEER_TOPK = 16
EPS = 1e-6
D_IN = 2560

LANES = 128
SUBLANES = 8
VMEM_LIMIT = 56 * 1024 * 1024

HG_SUB = 16
HG_SUPER = 128
HG_SAMPLE_SEQS = 4


def _cparams(sem):
    return pltpu.CompilerParams(dimension_semantics=sem, vmem_limit_bytes=VMEM_LIMIT)


_NN = (((1,), (0,)), ((), ()))
_NT = (((1,), (1,)), ((), ()))
_TN = (((0,), (0,)), ((), ()))


def _dg(a, b, dims):
    return lax.dot_general(a, b, dims, preferred_element_type=F32)


def _hi_lo(a):
    hi = a.astype(BF)
    lo = (a - hi.astype(F32)).astype(BF)
    return hi, lo


def _mm3(a, b, dims=_NN):
    ah, al = _hi_lo(a)
    bh, bl = _hi_lo(b)
    return (_dg(ah, bl, dims) + _dg(al, bh, dims)) + _dg(ah, bh, dims)


def _split3(a):
    a1 = a.astype(BF)
    r1 = a - a1.astype(F32)
    a2 = r1.astype(BF)
    a3 = (r1 - a2.astype(F32)).astype(BF)
    return a1, a2, a3


def _mm_x01(a, b01, dims=_NN):
    a1, a2, a3 = _split3(a)
    return (_dg(a3, b01, dims) + _dg(a2, b01, dims)) + _dg(a1, b01, dims)


def _mm_x01_2(a, b01, dims=_NN):
    ah, al = _hi_lo(a)
    return _dg(al, b01, dims) + _dg(ah, b01, dims)


def _mm_01x(a01, b, dims=_NN):
    b1, b2, b3 = _split3(b)
    return (_dg(a01, b3, dims) + _dg(a01, b2, dims)) + _dg(a01, b1, dims)


def _sigmoid(x):
    return jax.nn.sigmoid(x)


def _silu(x):
    return x * jax.nn.sigmoid(x)


def _gelu(x):
    return x * (0.5 * (1.0 + jnp.tanh(0.7978845608028654 * (x + 0.044715 * (x * x * x)))))


def _rms(x, w):
    return x * lax.rsqrt(jnp.mean(x * x, axis=-1, keepdims=True) + EPS) * w


def _seg_ones(width, seg):
    r = lax.broadcasted_iota(jnp.int32, (width, width), 0) // seg
    c = lax.broadcasted_iota(jnp.int32, (width, width), 1) // seg
    return jnp.where(r == c, 1.0, 0.0).astype(BF)


def _head_rms(x, gain):
    ms = _mm_x01(x * x, _seg_ones(x.shape[-1], HEAD_DIM)) * (1.0 / HEAD_DIM)
    return x * lax.rsqrt(ms + EPS) * gain


def _rope(x, cos_f, sin_s):
    w = x.shape[-1]
    lane = lax.broadcasted_iota(jnp.int32, x.shape, x.ndim - 1) % HEAD_DIM
    rot = jnp.where(lane < HEAD_DIM // 2,
                    pltpu.roll(x, w - HEAD_DIM // 2, x.ndim - 1),
                    pltpu.roll(x, HEAD_DIM // 2, x.ndim - 1))
    return x * cos_f + rot * sin_s


def _lane_head_mask(width, j):
    lane = lax.broadcasted_iota(jnp.int32, (1, width), 1) // HEAD_DIM
    return jnp.where(lane == j, 1.0, 0.0)


def _ada_body(c_ref, w_ref, b_ref, o_ref):
    c = _silu(c_ref[...])
    o_ref[...] = _mm3(c, w_ref[...]) + b_ref[...]


def _ada(c_all, w_ada, b_ada):
    n = c_all.shape[0]
    cb = 1536
    return pl.pallas_call(
        _ada_body,
        grid=(DEPTH, 6 * D_MODEL // cb),
        in_specs=[
            pl.BlockSpec((n, D_MODEL), lambda l, j: (0, 0)),
            pl.BlockSpec((None, D_MODEL, cb), lambda l, j: (l, 0, j)),
            pl.BlockSpec((None, 1, cb), lambda l, j: (l, 0, j)),
        ],
        out_specs=pl.BlockSpec((None, n, cb), lambda l, j: (l, 0, j)),
        out_shape=jax.ShapeDtypeStruct((DEPTH, n, 6 * D_MODEL), F32),
        compiler_params=_cparams(("arbitrary", "arbitrary")),
        name="ada",
    )(c_all, w_ada, b_ada.reshape(DEPTH, 1, 6 * D_MODEL))


class _Rows:
    def __init__(self, nb, nt, tn, per_row_mod):
        self.nb, self.nt, self.tn, self.per_row_mod = nb, nt, tn, per_row_mod

    def spec(self, width, col=0):
        nt = self.nt
        return pl.BlockSpec((self.tn, width), lambda b, t: (b * nt + t, col))

    def mod_spec(self, j):
        if self.per_row_mod:
            nt = self.nt
            return pl.BlockSpec((self.tn, D_MODEL), lambda b, t: (b * nt + t, j))
        return pl.BlockSpec((None, 1, D_MODEL), lambda b, t: (b, 0, j))

    @property
    def grid(self):
        return (self.nb, self.nt)

    @property
    def rows(self):
        return self.nb * self.nt * self.tn


def _const_spec(shape):
    nd = len(shape)
    return pl.BlockSpec(shape, lambda b, t: (0,) * nd)


def _in_body(x_ref, nw_ref, sh_ref, sc_ref, w_ref, z_ref):
    h = _rms(x_ref[...], nw_ref[...])
    h = h * (1.0 + sc_ref[...]) + sh_ref[...]
    z_ref[...] = jnp.dot(h.astype(BF), w_ref[...], preferred_element_type=F32)


def _in_proj(rows, x, mod, nw, w_bf):
    return pl.pallas_call(
        _in_body,
        grid=rows.grid,
        in_specs=[rows.spec(D_MODEL), _const_spec((1, D_MODEL)),
                  rows.mod_spec(0), rows.mod_spec(1), _const_spec((D_MODEL, D_IN))],
        out_specs=rows.spec(D_IN),
        out_shape=jax.ShapeDtypeStruct((rows.rows, D_IN), F32),
        compiler_params=_cparams(("arbitrary", "arbitrary")),
        name="in_proj",
    )(x, nw, mod, mod, w_bf)


def _out_body(x_ref, ya_ref, yb_ref, yc_ref, yd_ref, on_ref, w_ref, g1_ref, nf_ref,
              sh_ref, sc_ref, x1_ref, h2_ref):
    ys = []
    for g, r in enumerate((ya_ref, yb_ref, yc_ref, yd_ref)):
        ys.append(_rms(r[...], on_ref[:, g * W_GROUP:(g + 1) * W_GROUP]).astype(BF))
    o = jnp.dot(jnp.concatenate(ys, axis=-1), w_ref[...], preferred_element_type=F32)
    x1 = x_ref[...] + g1_ref[...] * o
    x1_ref[...] = x1
    h2 = _rms(x1, nf_ref[...])
    h2_ref[...] = h2 * (1.0 + sc_ref[...]) + sh_ref[...]


def _out_proj(rows, x, ys, mod, on, w_bf, nf):
    sds = jax.ShapeDtypeStruct((rows.rows, D_MODEL), F32)
    return pl.pallas_call(
        _out_body,
        grid=rows.grid,
        in_specs=[rows.spec(D_MODEL)] + [rows.spec(W_GROUP)] * 4
        + [_const_spec((1, D_MODEL)), _const_spec((D_MODEL, D_MODEL)), rows.mod_spec(2),
           _const_spec((1, D_MODEL)), rows.mod_spec(3), rows.mod_spec(4)],
        out_specs=[rows.spec(D_MODEL), rows.spec(D_MODEL)],
        out_shape=[sds, sds],
        compiler_params=_cparams(("arbitrary", "arbitrary")),
        name="out_proj",
    )(x, *ys, on, w_bf, mod, nf, mod, mod)


def _lru_gates(xc, wa, wx, ba, bx, lam):
    r = _sigmoid(_mm3(xc, wa) + ba)
    ig = _sigmoid(_mm3(xc, wx) + bx)
    nl = -lam
    sp = jnp.maximum(nl, 0.0) + jnp.log1p(jnp.exp(-jnp.abs(nl)))
    log_a = (-LRU_C) * sp * r
    a = jnp.exp(log_a)
    x2 = 2.0 * log_a
    em1 = jnp.tanh(0.5 * x2) * (jnp.exp(x2) + 1.0)
    mult = jnp.sqrt(jnp.maximum(-em1, LRU_FLOOR))
    return a, mult * (ig * xc)


def _lru_prompt_body(z_ref, cw_ref, cb_ref, wa_ref, wx_ref, ba_ref, bx_ref, lam_ref,
                     y_ref, hfin_ref, tail_ref, xbuf, hcar, *, tb):
    t = pl.program_id(1)

    @pl.when(t == 0)
    def _():
        xbuf[0:SUBLANES, :] = jnp.zeros((SUBLANES, W_GROUP), F32)
        hcar[...] = jnp.zeros_like(hcar)

    xa = z_ref[:, 0:W_GROUP]
    ga = z_ref[:, W_GROUP:2 * W_GROUP]
    xbuf[SUBLANES:SUBLANES + tb, :] = xa
    xc = cb_ref[...] + cw_ref[3:4, :] * xa
    for k in range(CONV_W - 1):
        xc = xc + cw_ref[k:k + 1, :] * xbuf[pl.ds(SUBLANES - (CONV_W - 1) + k, tb), :]
    xbuf[0:SUBLANES, :] = xa[tb - SUBLANES:tb, :]
    tail_ref[...] = xa[tb - SUBLANES:tb, :]

    a, b = _lru_gates(xc, wa_ref[...], wx_ref[...], ba_ref[...], bx_ref[...], lam_ref[...])
    row = lax.broadcasted_iota(jnp.int32, (tb, W_GROUP), 0)
    d = 1
    while d < tb:
        a_s = pltpu.roll(a, d, 0)
        b_s = pltpu.roll(b, d, 0)
        m = row >= d
        b = jnp.where(m, a * b_s + b, b)
        a = jnp.where(m, a * a_s, a)
        d *= 2
    h = b + a * hcar[0:1, :]
    hl = h[tb - 1:tb, :]
    hcar[...] = jnp.broadcast_to(hl, hcar.shape)
    hfin_ref[...] = hl
    y_ref[...] = h * _gelu(ga)


def _lru_prompt(z, nb, t_len, p, tb=512):
    nt = t_len // tb
    w2 = (1, W_GROUP)
    return pl.pallas_call(
        functools.partial(_lru_prompt_body, tb=tb),
        grid=(nb, nt),
        in_specs=[pl.BlockSpec((tb, 2 * W_GROUP), lambda b, t: (b * nt + t, 0)),
                  _const_spec((CONV_W, W_GROUP)), _const_spec(w2),
                  _const_spec((W_GROUP, W_GROUP)), _const_spec((W_GROUP, W_GROUP)),
                  _const_spec(w2), _const_spec(w2), _const_spec(w2)],
        out_specs=[pl.BlockSpec((tb, W_GROUP), lambda b, t: (b * nt + t, 0)),
                   pl.BlockSpec((None, 1, W_GROUP), lambda b, t: (b, 0, 0)),
                   pl.BlockSpec((None, SUBLANES, W_GROUP), lambda b, t: (b, 0, 0))],
        out_shape=[jax.ShapeDtypeStruct((nb * t_len, W_GROUP), F32),
                   jax.ShapeDtypeStruct((nb, 1, W_GROUP), F32),
                   jax.ShapeDtypeStruct((nb, SUBLANES, W_GROUP), F32)],
        scratch_shapes=[pltpu.VMEM((tb + SUBLANES, W_GROUP), F32),
                        pltpu.VMEM((SUBLANES, W_GROUP), F32)],
        compiler_params=_cparams(("arbitrary", "arbitrary")),
        name="lru_prompt",
    )(z, p["conv_w"], p["conv_b"], p["wa_bd"], p["wx_bd"], p["lru_ba"], p["lru_bx"], p["lru_lambda"])


def _sgu_prompt_body(z_ref, gain_ref, ws_ref, bias_ref, y_ref, *, tb):
    u = _gelu(z_ref[:, 0:W_GROUP])
    v = _rms(_gelu(z_ref[:, W_GROUP:2 * W_GROUP]), gain_ref[...])
    masks = [_lane_head_mask(W_GROUP, g) for g in range(W_GROUP // HEAD_DIM)]
    ws = ws_ref[...]
    bias = bias_ref[...]
    for j in range(tb // CHUNK_B):
        vj = v[j * CHUNK_B:(j + 1) * CHUNK_B, :]
        rhs = jnp.concatenate([vj * m for m in masks], axis=0)
        mix = _mm3(ws, rhs) + bias
        y_ref[j * CHUNK_B:(j + 1) * CHUNK_B, :] = u[j * CHUNK_B:(j + 1) * CHUNK_B, :] * mix


def _sgu_prompt(z, nb, t_len, p, tb=512):
    nt = t_len // tb
    nh = W_GROUP // HEAD_DIM
    return pl.pallas_call(
        functools.partial(_sgu_prompt_body, tb=tb),
        grid=(nb, nt),
        in_specs=[pl.BlockSpec((tb, 2 * W_GROUP), lambda b, t: (b * nt + t, 1)),
                  _const_spec((1, W_GROUP)), _const_spec((CHUNK_B, nh * CHUNK_B)),
                  _const_spec((CHUNK_B, W_GROUP))],
        out_specs=pl.BlockSpec((tb, W_GROUP), lambda b, t: (b * nt + t, 0)),
        out_shape=jax.ShapeDtypeStruct((nb * t_len, W_GROUP), F32),
        compiler_params=_cparams(("arbitrary", "arbitrary")),
        name="sgu_prompt",
    )(z, p["sgu_norm"], p["sgu_ws_cat"], p["sgu_bias"])


def _ab_sample_body(z_ref, h0_ref, buf_ref, cw_ref, cb_ref, wa_ref, wx_ref, ba_ref, bx_ref,
                    lam_ref, gain_ref, mw_ref, mb_ref,
                    ya_ref, hfin_ref, nbuf_ref, yb_ref, vrow_ref, *, nb, nt):
    xa = z_ref[:, 0:W_GROUP]
    ga = z_ref[:, W_GROUP:2 * W_GROUP]
    slabs = [buf_ref[k] for k in range(CONV_W - 1)] + [xa[t * nb:(t + 1) * nb, :] for t in range(nt)]
    xcs = []
    for t in range(nt):
        xc = cb_ref[...]
        for k in range(CONV_W):
            xc = xc + cw_ref[k:k + 1, :] * slabs[t + k]
        xcs.append(xc)
    for k in range(CONV_W - 1):
        nbuf_ref[k] = slabs[nt + k]
    xc = jnp.concatenate(xcs, axis=0)
    a, b = _lru_gates(xc, wa_ref[...], wx_ref[...], ba_ref[...], bx_ref[...], lam_ref[...])
    h = h0_ref[...]
    hs = []
    for t in range(nt):
        h = a[t * nb:(t + 1) * nb, :] * h + b[t * nb:(t + 1) * nb, :]
        hs.append(h)
    hfin_ref[...] = h
    ya_ref[...] = jnp.concatenate(hs, axis=0) * _gelu(ga)

    u = _gelu(z_ref[:, 2 * W_GROUP:3 * W_GROUP])
    v = _rms(_gelu(z_ref[:, 3 * W_GROUP:4 * W_GROUP]), gain_ref[...])
    vrow_ref[...] = v
    for t in range(nt):
        mix = mb_ref[t]
        for s in range(t + 1):
            mix = mix + mw_ref[t, s] * v[s * nb:(s + 1) * nb, :]
        yb_ref[t * nb:(t + 1) * nb, :] = u[t * nb:(t + 1) * nb, :] * mix


def _ab_sample(z, h0, buf, p, nb, nt):
    n = nb * nt
    full = lambda shape: pl.BlockSpec(shape, lambda i: (0,) * len(shape))
    w2 = (1, W_GROUP)
    sd = lambda *s: jax.ShapeDtypeStruct(s, F32)
    return pl.pallas_call(
        functools.partial(_ab_sample_body, nb=nb, nt=nt),
        grid=(1,),
        in_specs=[full((n, 4 * W_GROUP)), full((nb, W_GROUP)), full((CONV_W - 1, nb, W_GROUP)),
                  full((CONV_W, W_GROUP)), full(w2), full((W_GROUP, W_GROUP)),
                  full((W_GROUP, W_GROUP)), full(w2), full(w2), full(w2), full(w2),
                  full((nt, nt, 1, W_GROUP)), full((nt, 1, W_GROUP))],
        out_specs=[full((n, W_GROUP)), full((nb, W_GROUP)), full((CONV_W - 1, nb, W_GROUP)),
                   full((n, W_GROUP)), full((n, W_GROUP))],
        out_shape=[sd(n, W_GROUP), sd(nb, W_GROUP), sd(CONV_W - 1, nb, W_GROUP),
                   sd(n, W_GROUP), sd(n, W_GROUP)],
        compiler_params=_cparams(("arbitrary",)),
        name="ab_sample",
    )(z, h0, buf, p["conv_w"], p["conv_b"], p["wa_bd"], p["wx_bd"], p["lru_ba"], p["lru_bx"],
      p["lru_lambda"], p["sgu_norm"], p["sgu_mw_s"], p["sgu_mb_s"])


def _swa_prompt_body(sink_ref, z_ref, cos_ref, sin_ref, qg_ref, kg_ref,
                     y_ref, klast_ref, vlast_ref, kprev, vprev, *, tb):
    t = pl.program_id(1)

    @pl.when(t == 0)
    def _():
        kprev[...] = jnp.zeros_like(kprev)
        vprev[...] = jnp.zeros_like(vprev)

    cos = cos_ref[...]
    sin = sin_ref[...]
    q = _head_rms(z_ref[:, 0:2 * LANES], qg_ref[...])
    q = _rope(q, jnp.concatenate([cos, cos], axis=-1), jnp.concatenate([sin, sin], axis=-1))
    k = _rope(_head_rms(z_ref[:, 2 * LANES:3 * LANES], kg_ref[...]), cos, sin)
    v = z_ref[:, 3 * LANES:4 * LANES]
    masks = [_lane_head_mask(LANES, j) for j in range(N_KV_C)]
    r = lax.broadcasted_iota(jnp.int32, (WINDOW, 2 * WINDOW), 0)
    c = lax.broadcasted_iota(jnp.int32, (WINDOW, 2 * WINDOW), 1)
    diff = r + WINDOW - c
    band = (diff >= 0) & (diff < WINDOW)
    nsub = tb // WINDOW
    for i in range(nsub):
        sl = slice(i * WINDOW, (i + 1) * WINDOW)
        ki, vi = k[sl, :], v[sl, :]
        kk = jnp.concatenate([kprev[...], ki], axis=0)
        vv = jnp.concatenate([vprev[...], vi], axis=0)
        first = (t * nsub + i) == 0
        ok = band & ((c >= WINDOW) | jnp.logical_not(first))
        outs = []
        for g in range(N_Q_C // N_KV_C):
            qg = q[sl, g * LANES:(g + 1) * LANES]
            og = jnp.zeros((WINDOW, LANES), F32)
            for j in range(N_KV_C):
                s = _mm3(qg * masks[j], kk, _NT) * (HEAD_DIM ** -0.5)
                s = jnp.where(ok, s, NEG_BIG)
                sink = sink_ref[j * (N_Q_C // N_KV_C) + g]
                m = jnp.maximum(jnp.max(s, axis=-1, keepdims=True), sink)
                e = jnp.exp(s - m)
                den = jnp.sum(e, axis=-1, keepdims=True) + jnp.exp(sink - m)
                og = og + _mm3(e / den, vv * masks[j])
            outs.append(og)
        y_ref[sl, :] = jnp.concatenate(outs, axis=-1)
        kprev[...] = ki
        vprev[...] = vi
    klast_ref[...] = k[tb - WINDOW:tb, :]
    vlast_ref[...] = v[tb - WINDOW:tb, :]


def _swa_prompt(z, nb, t_len, p, cos_t, sin_t, tb=512):
    nt = t_len // tb
    sd = lambda *s: jax.ShapeDtypeStruct(s, F32)
    return pl.pallas_call(
        functools.partial(_swa_prompt_body, tb=tb),
        grid=(nb, nt),
        in_specs=[pl.BlockSpec(memory_space=pltpu.SMEM),
                  pl.BlockSpec((tb, 4 * LANES), lambda b, t: (b * nt + t, 2)),
                  pl.BlockSpec((tb, LANES), lambda b, t: (t, 0)),
                  pl.BlockSpec((tb, LANES), lambda b, t: (t, 0)),
                  _const_spec((1, 2 * LANES)), _const_spec((1, LANES))],
        out_specs=[pl.BlockSpec((tb, 2 * LANES), lambda b, t: (b * nt + t, 0)),
                   pl.BlockSpec((None, WINDOW, LANES), lambda b, t: (b, 0, 0)),
                   pl.BlockSpec((None, WINDOW, LANES), lambda b, t: (b, 0, 0))],
        out_shape=[sd(nb * t_len, 2 * LANES), sd(nb, WINDOW, LANES), sd(nb, WINDOW, LANES)],
        scratch_shapes=[pltpu.VMEM((WINDOW, LANES), F32), pltpu.VMEM((WINDOW, LANES), F32)],
        compiler_params=_cparams(("arbitrary", "arbitrary")),
        name="swa_prompt",
    )(p["sinks"], z, cos_t, sin_t, p["q_gain"], p["k_gain"])


def _swa_sample_body(z_ref, cos_ref, sin_ref, qg_ref, kg_ref, sink_ref, kb_ref, vb_ref,
                     y_ref, knew_ref, *, bb, nt):
    seg = _seg_ones(LANES, HEAD_DIM)
    kb = kb_ref[...]
    vb = vb_ref[...]
    nkeys = kb.shape[0]
    kidx = lax.broadcasted_iota(jnp.int32, kb.shape, 0)
    qs, ks, vs = [], [], []
    for t in range(nt):
        zt = z_ref[t]
        cos, sin = cos_ref[t], sin_ref[t]
        q = _head_rms(zt[:, 0:2 * LANES], qg_ref[...])
        qs.append(_rope(q, jnp.concatenate([cos, cos], axis=-1), jnp.concatenate([sin, sin], axis=-1)))
        kt = _rope(_head_rms(zt[:, 2 * LANES:3 * LANES], kg_ref[...]), cos, sin)
        ks.append(kt)
        vs.append(zt[:, 3 * LANES:4 * LANES])
        knew_ref[t] = kt
    scale = HEAD_DIM ** -0.5
    for t in range(nt):
        outs = []
        for g in range(N_Q_C // N_KV_C):
            qtg = qs[t][:, g * LANES:(g + 1) * LANES]
            prod = (qtg[None, :, :] * kb).reshape(nkeys * bb, LANES)
            sb = (_mm_x01(prod, seg) * scale).reshape(nkeys, bb, LANES)
            sb = jnp.where(kidx >= t + 1 + (nkeys - WINDOW), sb, NEG_BIG)
            sn = [_mm_x01(qtg * ks[s], seg) * scale for s in range(t + 1)]
            sink = sink_ref[g]
            m = jnp.maximum(jnp.max(sb, axis=0), sink)
            for x in sn:
                m = jnp.maximum(m, x)
            eb = jnp.exp(sb - m[None, :, :])
            den = jnp.sum(eb, axis=0) + jnp.exp(sink - m)
            num = jnp.sum(eb * vb, axis=0)
            for s, x in enumerate(sn):
                en = jnp.exp(x - m)
                den = den + en
                num = num + en * vs[s]
            outs.append(num / den)
        y_ref[t] = jnp.concatenate(outs, axis=-1)


def _swa_sample(z3, cos_s, sin_s, p, kb_t, vb_t, nb, nt, bb=16):
    sd = lambda *s: jax.ShapeDtypeStruct(s, F32)
    nk = kb_t.shape[0]
    c3 = lambda shape: pl.BlockSpec(shape, lambda i: (0,) * len(shape))
    return pl.pallas_call(
        functools.partial(_swa_sample_body, bb=bb, nt=nt),
        grid=(nb // bb,),
        in_specs=[pl.BlockSpec((nt, bb, 4 * LANES), lambda i: (0, i, 2)),
                  c3((nt, 1, LANES)), c3((nt, 1, LANES)), c3((1, 2 * LANES)), c3((1, LANES)),
                  c3((N_Q_C // N_KV_C, 1, LANES)),
                  pl.BlockSpec((nk, bb, LANES), lambda i: (0, i, 0)),
                  pl.BlockSpec((nk, bb, LANES), lambda i: (0, i, 0))],
        out_specs=[pl.BlockSpec((nt, bb, 2 * LANES), lambda i: (0, i, 0)),
                   pl.BlockSpec((nt, bb, LANES), lambda i: (0, i, 0))],
        out_shape=[sd(nt, nb, 2 * LANES), sd(nt, nb, LANES)],
        compiler_params=_cparams(("arbitrary",)),
        name="swa_sample",
    )(z3, cos_s, sin_s, p["q_gain"], p["k_gain"], p["sink_lanes"], kb_t, vb_t)


def _hgrn_body(*refs, rows, nq, has_init, n_valid):
    if has_init:
        (zqf_ref, zig_ref, lb_ref, gn_ref, s0_ref, y_ref, sfin_ref,
         st, qq_s, qe_s, kk_s, kd_s, bc_s, bl_s, v_s, o_s) = refs
    else:
        (zqf_ref, zig_ref, lb_ref, gn_ref, y_ref, sfin_ref,
         st, qq_s, qe_s, kk_s, kd_s, bc_s, bl_s, v_s, o_s) = refs
        s0_ref = None
    t = pl.program_id(1)

    @pl.when(t == 0)
    def _():
        if has_init:
            st[...] = s0_ref[...]
        else:
            st[...] = jnp.zeros_like(st)

    sup = HG_SUPER
    n_super = max(rows // sup, 1)
    real = min(rows, sup)
    lb = lb_ref[...]
    seg_heads = _seg_ones(W_GROUP, HEAD_DIM)
    bd_mask = seg_heads.astype(F32)
    ri = lax.broadcasted_iota(jnp.int32, (sup, sup), 0)
    ci = lax.broadcasted_iota(jnp.int32, (sup, sup), 1)
    same = (ri // HG_SUB) == (ci // HG_SUB)
    tri_bd = jnp.where(same & (ci <= ri), 1.0, 0.0).astype(BF)
    ones_bd = jnp.where(same, 1.0, 0.0).astype(BF)
    pr = lax.broadcasted_iota(jnp.int32, (HG_SUB * HG_SUB, W_GROUP), 0)
    causal = jnp.where((pr % HG_SUB) >= (pr // HG_SUB), 1.0, 0.0)

    def pad(x):
        if real == sup:
            return x
        return jnp.concatenate([x, jnp.zeros((sup - real, x.shape[1]), F32)], axis=0)

    for sb in range(n_super):
        rs = slice(sb * sup, sb * sup + real)
        for q in range(nq):
            zf = zqf_ref[q, rs, W_GROUP:2 * W_GROUP]
            f = lb + (1.0 - lb) * _sigmoid(zf)
            logf = jnp.log(jnp.maximum(f, 1e-30))
            kk = (1.0 - lb) * _sigmoid(-zf)
            if n_valid < real:
                valid = lax.broadcasted_iota(jnp.int32, (real, W_GROUP), 0) < n_valid
                logf = jnp.where(valid, logf, 0.0)
                kk = jnp.where(valid, kk, 0.0)
            logf, kk = pad(logf), pad(kk)
            qq = pad(_silu(zqf_ref[q, rs, 0:W_GROUP]))
            v = pad(zig_ref[q, rs, 0:W_GROUP])
            bc = _mm_01x(tri_bd, logf)
            bl = _mm_01x(ones_bd, logf)
            qq_s[q] = qq
            kk_s[q] = kk
            bc_s[q] = bc
            bl_s[q] = bl
            qe_s[q] = qq * jnp.exp(bc)
            kd_s[q] = kk * jnp.exp(bl - bc)
            v_s[q] = v

        n_sub = -(-real // HG_SUB)

        def sub(c, carry):
            r0 = pl.multiple_of(c * HG_SUB, HG_SUB)
            rsub = pl.ds(r0, HG_SUB)
            pms = []
            for q in range(nq):
                bc_c = bc_s[q, rsub, :]
                qq_c = qq_s[q, rsub, :]
                kk_c = kk_s[q, rsub, :]
                parts = []
                for s in range(HG_SUB):
                    e = jnp.exp(jnp.minimum(bc_c - bc_c[s:s + 1, :], 0.0))
                    parts.append(qq_c * e * kk_c[s:s + 1, :])
                pms.append(jnp.concatenate(parts, axis=0) * causal)
            atts = [_mm_x01_2(pm, seg_heads) for pm in pms]
            inter = [_mm3(qe_s[q, rsub, :], st[q], _NT) for q in range(nq)]
            upd = [_mm3(v_s[q, rsub, :], kd_s[q, rsub, :], _TN) for q in range(nq)]
            for q in range(nq):
                v_c = v_s[q, rsub, :]
                o_c = inter[q]
                for s in range(HG_SUB):
                    o_c = o_c + atts[q][s * HG_SUB:(s + 1) * HG_SUB, :] * v_c[s:s + 1, :]
                o_s[q, pl.ds(pl.multiple_of(sb * sup + r0, HG_SUB), HG_SUB), :] = o_c
                decay = jnp.exp(bl_s[q, pl.ds(r0, 1), :])
                st[q] = st[q] * decay + upd[q] * bd_mask
            return carry

        lax.fori_loop(0, n_sub, sub, 0)

    for q in range(nq):
        o = o_s[q, 0:rows, :]
        zg = zig_ref[q, :, W_GROUP:2 * W_GROUP]
        y_ref[q] = _head_rms(o, gn_ref[...]) * _silu(zg)
    sfin_ref[...] = st[...]


def _hgrn(z3, qf_col, ig_col, nq, rows, p, s0=None, n_valid=None):
    nseq, t_len, _ = z3.shape
    nt = t_len // rows
    n_valid = rows if n_valid is None else n_valid
    sd = lambda *s: jax.ShapeDtypeStruct(s, F32)
    in_specs = [pl.BlockSpec((nq, rows, 2 * W_GROUP), lambda b, t: (b, t, qf_col)),
                pl.BlockSpec((nq, rows, 2 * W_GROUP), lambda b, t: (b, t, ig_col)),
                _const_spec((1, W_GROUP)), _const_spec((1, W_GROUP))]
    args = [z3, z3, p["hgrn_lb"], p["hgrn_gn"]]
    if s0 is not None:
        in_specs.append(pl.BlockSpec((nq, W_GROUP, W_GROUP), lambda b, t: (b, 0, 0)))
        args.append(s0)
    scr_rows = max(rows, HG_SUPER)
    sup_shape = pltpu.VMEM((nq, HG_SUPER, W_GROUP), F32)
    return pl.pallas_call(
        functools.partial(_hgrn_body, rows=rows, nq=nq, has_init=s0 is not None, n_valid=n_valid),
        grid=(nseq // nq, nt),
        in_specs=in_specs,
        out_specs=[pl.BlockSpec((nq, rows, W_GROUP), lambda b, t: (b, t, 0)),
                   pl.BlockSpec((nq, W_GROUP, W_GROUP), lambda b, t: (b, 0, 0))],
        out_shape=[sd(nseq, t_len, W_GROUP), sd(nseq, W_GROUP, W_GROUP)],
        scratch_shapes=[pltpu.VMEM((nq, W_GROUP, W_GROUP), F32)] + [sup_shape] * 7
        + [pltpu.VMEM((nq, scr_rows, W_GROUP), F32)],
        compiler_params=_cparams(("arbitrary", "arbitrary")),
        name="hgrn",
    )(*args)


def _oddeven_merge_sort(n):
    pairs = []

    def merge(lo, m, r):
        step = r * 2
        if step < m:
            merge(lo, m, step)
            merge(lo + r, m, step)
            for i in range(lo + r, lo + m - r, step):
                pairs.append((i, i + r))
        else:
            pairs.append((lo, lo + r))

    def sort(lo, m):
        if m > 1:
            h = m // 2
            sort(lo, h)
            sort(lo + h, h)
            merge(lo, m, 1)

    sort(0, n)
    return pairs


_NET16 = _oddeven_merge_sort(PEER_TOPK)
_CELLS = [(a, b) for a in range(PEER_TOPK) for b in range(PEER_TOPK)
          if (a + 1) * (b + 1) <= PEER_TOPK]


def _beats(va, ia, vb, ib):
    return (va > vb) | ((va == vb) & (ia < ib))


def _cmpx(items, i, j):
    (va, ia), (vb, ib) = items[i], items[j]
    gt = _beats(va, ia, vb, ib)
    items[i] = (jnp.where(gt, va, vb), jnp.where(gt, ia, ib))
    items[j] = (jnp.where(gt, vb, va), jnp.where(gt, ib, ia))


def _merge_top(a_items, b_items):
    n = len(a_items)
    items = []
    for i in range(n):
        (va, ia), (vb, ib) = a_items[i], b_items[n - 1 - i]
        gt = _beats(va, ia, vb, ib)
        items.append((jnp.where(gt, va, vb), jnp.where(gt, ia, ib)))
    d = n // 2
    while d >= 1:
        for i in range(n):
            if not (i & d):
                _cmpx(items, i, i + d)
        d //= 2
    return items


def _top16_sorted(load_key, n_keys, shape):
    stack = []
    for g in range(n_keys // PEER_TOPK):
        items = [(load_key(g * PEER_TOPK + j), jnp.full(shape, float(g * PEER_TOPK + j), F32))
                 for j in range(PEER_TOPK)]
        for i, j in _NET16:
            _cmpx(items, i, j)
        level = 0
        while stack and stack[-1][0] == level:
            _, other = stack.pop()
            items = _merge_top(other, items)
            level += 1
        stack.append((level, items))
    assert len(stack) == 1
    return stack[0][1]


def _route_body(h_ref, wqh_ref, wql_ref, kbh_ref, kbl_ref, e1_ref, e2_ref, g_ref,
                st_s, top_s, *, tn):
    hh, hl = _hi_lo(h_ref[...])
    wqh, wql = wqh_ref[...], wql_ref[...]
    qt = (_dg(wqh, hl, _NT) + _dg(wql, hh, _NT)) + _dg(wqh, hh, _NT)
    half = PEER_HEADS * PEER_DQ // 2
    for p in range(2):
        qp = qt[p * half:(p + 1) * half, :]
        qh, ql = _hi_lo(qp)
        kh, kl = kbh_ref[p], kbl_ref[p]
        st_s[p] = (_dg(kh, ql, _NN) + _dg(kl, qh, _NN)) + _dg(kh, qh, _NN)

    shape = (PEER_HEADS, LANES)

    def column(col, carry):
        c0 = pl.multiple_of(col * LANES, LANES)
        cs = pl.ds(c0, LANES)
        for p in range(2):
            top = _top16_sorted(lambda k: st_s[p, k * PEER_HEADS:(k + 1) * PEER_HEADS, cs],
                                PEER_KEYS, shape)
            for a, (va, ia) in enumerate(top):
                top_s[p, 0, a] = va
                top_s[p, 1, a] = ia
        v1 = [top_s[0, 0, a] for a in range(PEER_TOPK)]
        v2 = [top_s[1, 0, b] for b in range(PEER_TOPK)]
        cval = {cell: v1[cell[0]] + v2[cell[1]] for cell in _CELLS}
        rank_static = {cell: float((cell[0] + 1) * (cell[1] + 1) - 1) for cell in _CELLS}
        dyn = {cell: None for cell in _CELLS}
        for xi, x in enumerate(_CELLS):
            for y in _CELLS[xi + 1:]:
                if x[0] < y[0] and x[1] > y[1]:
                    xw = jnp.where(cval[x] >= cval[y], 1.0, 0.0)
                    dyn[y] = xw if dyn[y] is None else dyn[y] + xw
                    lose = 1.0 - xw
                    dyn[x] = lose if dyn[x] is None else dyn[x] + lose
        rank = {cell: (rank_static[cell] if dyn[cell] is None else dyn[cell] + rank_static[cell])
                for cell in _CELLS}
        c00 = cval[(0, 0)]
        ex = {}
        zsum = None
        for cell in _CELLS:
            if dyn[cell] is None:
                w = jnp.exp(cval[cell] - c00)
            else:
                w = jnp.where(rank[cell] < PEER_TOPK, jnp.exp(cval[cell] - c00), 0.0)
            ex[cell] = w
            zsum = w if zsum is None else zsum + w
        inv = 1.0 / zsum
        for k in range(PEER_TOPK):
            e1 = jnp.zeros(shape, F32)
            e2 = jnp.zeros(shape, F32)
            gk = jnp.zeros(shape, F32)
            for cell in _CELLS:
                lo = (cell[0] + 1) * (cell[1] + 1) - 1
                if lo > k:
                    continue
                if dyn[cell] is None:
                    if lo != k:
                        continue
                    e1, e2, gk = top_s[0, 1, cell[0]], top_s[1, 1, cell[1]], ex[cell]
                    continue
                hit = rank[cell] == float(k)
                e1 = jnp.where(hit, top_s[0, 1, cell[0]], e1)
                e2 = jnp.where(hit, top_s[1, 1, cell[1]], e2)
                gk = jnp.where(hit, ex[cell], gk)
            rows = slice(k * PEER_HEADS, (k + 1) * PEER_HEADS)
            e1_ref[rows, cs] = e1
            e2_ref[rows, cs] = e2
            g_ref[rows, cs] = gk * inv
        return carry

    lax.fori_loop(0, tn // LANES, column, 0)


def _route(h2, p, tn=512):
    n = h2.shape[0]
    nslots = PEER_TOPK * PEER_HEADS
    half = PEER_HEADS * PEER_DQ // 2
    nk = PEER_KEYS * PEER_HEADS
    sd = jax.ShapeDtypeStruct((nslots, n), F32)
    c1 = lambda shape: pl.BlockSpec(shape, lambda i: (0,) * len(shape))
    return pl.pallas_call(
        functools.partial(_route_body, tn=tn),
        grid=(n // tn,),
        in_specs=[pl.BlockSpec((tn, D_MODEL), lambda i: (i, 0)),
                  c1((2 * half, D_MODEL)), c1((2 * half, D_MODEL)),
                  c1((2, nk, half)), c1((2, nk, half))],
        out_specs=[pl.BlockSpec((nslots, tn), lambda i: (0, i))] * 3,
        out_shape=[sd, sd, sd],
        scratch_shapes=[pltpu.VMEM((2, nk, tn), F32),
                        pltpu.VMEM((2, 2, PEER_TOPK, PEER_HEADS, LANES), F32)],
        compiler_params=_cparams(("arbitrary",)),
        name="peer_route",
    )(h2, p["wq_t_hi"], p["wq_t_lo"], p["kbig_hi"], p["kbig_lo"])


PEER_E1_STEP = 8
PEER_STEP = PEER_E1_STEP * PEER_KEYS
PEER_BUILD_UNROLL = 2 * SUBLANES
PEER_PAIRS = PEER_KEYS // 2
HI16 = 0xFFFF0000
PEER_TOKEN_GROUPS = 2


def _peer_body(h_ref, e1_ref, e2_ref, g_ref, u_ref, v_ref, x1_ref, g2_ref, o_ref,
               gt_s, xb_s, acc_s, *, tn):
    cp = pl.program_id(2)

    @pl.when(cp == 0)
    def _():
        xb_s[...] = h_ref[...].astype(BF)
        acc_s[...] = jnp.zeros_like(acc_s)
        r = lax.broadcasted_iota(jnp.int32, (PEER_KEYS, LANES), 0)
        sub1 = jnp.where(r < PEER_PAIRS, 2 * r, 2 * (r - PEER_PAIRS) + 1).astype(F32)
        sub2 = r.astype(F32)

        def build(nb, carry):
            r0 = pl.multiple_of(nb * PEER_BUILD_UNROLL, PEER_BUILD_UNROLL)
            e1s = e1_ref[pl.ds(r0, PEER_BUILD_UNROLL), :]
            e2s = e2_ref[pl.ds(r0, PEER_BUILD_UNROLL), :]
            gs = g_ref[pl.ds(r0, PEER_BUILD_UNROLL), :]
            for j in range(PEER_BUILD_UNROLL):
                pt = jnp.where(sub1 == e1s[j:j + 1, :], 1.0, 0.0).astype(BF)
                qg = jnp.where(sub2 == e2s[j:j + 1, :], gs[j:j + 1, :], 0.0).astype(BF)
                gn = _dg(pt, qg, _NT).astype(BF).astype(F32)
                even = lax.bitcast_convert_type(gn[0:PEER_PAIRS, :], jnp.uint32)
                odd = lax.bitcast_convert_type(gn[PEER_PAIRS:PEER_KEYS, :], jnp.uint32)
                row = pl.multiple_of((r0 + j) * PEER_PAIRS, PEER_PAIRS)
                gt_s[pl.ds(row, PEER_PAIRS), :] = (odd & jnp.uint32(HI16)) | (even >> 16)
            return carry

        lax.fori_loop(0, tn // PEER_BUILD_UNROLL, build, 0)

    tg = tn // PEER_TOKEN_GROUPS
    groups = [slice(grp * tg, (grp + 1) * tg) for grp in range(PEER_TOKEN_GROUPS)]
    pre = [_dg(xb_s[rows, :], u_ref[...], _NT) for rows in groups]
    outs = []
    for grp, rows in enumerate(groups):
        gates = []
        for j in range(PEER_E1_STEP // 2):
            first = grp * tg * PEER_PAIRS + (PEER_E1_STEP // 2) * cp + j
            w = gt_s[pl.ds(first, tg, stride=PEER_PAIRS), :]
            gates.append(lax.bitcast_convert_type(w << 16, F32))
            gates.append(lax.bitcast_convert_type(w & jnp.uint32(HI16), F32))
        ga = (_gelu(pre[grp]) * jnp.concatenate(gates, axis=-1)).astype(BF)
        outs.append(jnp.dot(ga, v_ref[...], preferred_element_type=F32))
    for rows, o in zip(groups, outs):
        acc_s[rows, :] += o

    @pl.when(cp == pl.num_programs(2) - 1)
    def _():
        o_ref[...] = x1_ref[...] + g2_ref[...] * acc_s[...]


def _peer(rows, h2, e1, e2, g, u_bf, v_bf, x1, mod):
    tn = rows.tn
    nt = rows.nt
    ncp = PEER_KEYS * PEER_KEYS // PEER_STEP
    tok = lambda width: pl.BlockSpec((tn, width), lambda b, t, c: (b * nt + t, 0))
    if rows.per_row_mod:
        g2_spec = pl.BlockSpec((tn, D_MODEL), lambda b, t, c: (b * nt + t, 5))
    else:
        g2_spec = pl.BlockSpec((None, 1, D_MODEL), lambda b, t, c: (b, 0, 5))
    return pl.pallas_call(
        functools.partial(_peer_body, tn=tn),
        grid=(rows.nb, rows.nt, ncp),
        in_specs=[tok(D_MODEL), tok(LANES), tok(LANES), tok(LANES),
                  pl.BlockSpec((PEER_STEP, D_MODEL), lambda b, t, c: (c, 0)),
                  pl.BlockSpec((PEER_STEP, D_MODEL), lambda b, t, c: (c, 0)),
                  tok(D_MODEL), g2_spec],
        out_specs=tok(D_MODEL),
        out_shape=jax.ShapeDtypeStruct((rows.rows, D_MODEL), F32),
        scratch_shapes=[pltpu.VMEM((tn * PEER_PAIRS, LANES), jnp.uint32),
                        pltpu.VMEM((tn, D_MODEL), BF),
                        pltpu.VMEM((tn, D_MODEL), F32)],
        compiler_params=_cparams(("arbitrary", "arbitrary", "arbitrary")),
        name="peer_experts",
    )(h2, e1, e2, g, u_bf, v_bf, x1, mod)


_Q_HEAD_ORDER = (0, 2, 1, 3)


def _q_perm():
    return np.concatenate([np.arange(h * HEAD_DIM, (h + 1) * HEAD_DIM) for h in _Q_HEAD_ORDER])


def _block_diag(w):
    nblk, bi, bo = w.shape
    out = jnp.zeros((nblk * bi, nblk * bo), w.dtype)
    for h in range(nblk):
        out = out.at[h * bi:(h + 1) * bi, h * bo:(h + 1) * bo].set(w[h])
    return out


def _rope_tables(pos):
    half = HEAD_DIM // 2
    freqs = ROPE_THETA ** (-jnp.arange(half, dtype=F32) / half)
    ang = pos.astype(F32)[:, None] * freqs[None, :]
    cos, sin = jnp.cos(ang), jnp.sin(ang)
    cos_h = jnp.concatenate([cos, cos], axis=-1)
    sin_h = jnp.concatenate([-sin, sin], axis=-1)
    reps = LANES // HEAD_DIM
    return jnp.tile(cos_h, (1, reps)), jnp.tile(sin_h, (1, reps))


def _layer_params(i, nt_s, lower_bound, w_in, conv_w, conv_b, lru_wa, lru_ba, lru_wx, lru_bx,
                  lru_lambda, sgu_norm, sgu_ws, sgu_b, q_norm, k_norm, sinks, hgrn_gnorm,
                  out_norm, w_out, norm_ffn, norm_mix, peer_wq, peer_keys, peer_u, peer_v):
    row = lambda v: v.reshape(1, -1)
    qperm = _q_perm()
    q0 = 4 * W_GROUP
    in_perm = np.arange(D_IN)
    in_perm[q0:q0 + W_GROUP] = q0 + qperm
    c0 = 2 * W_GROUP
    mix_perm = np.arange(N_GROUPS * W_GROUP)
    mix_perm[c0:c0 + W_GROUP] = c0 + qperm
    nh = W_GROUP // HEAD_DIM
    causal = jnp.tril(jnp.ones((CHUNK_B, CHUNK_B), F32))
    ws_c = sgu_ws[i] * causal
    kbig = jnp.zeros((2, PEER_KEYS, PEER_HEADS, PEER_HEADS, PEER_DQ // 2), F32)
    for h in range(PEER_HEADS):
        kbig = kbig.at[:, :, h, h, :].set(peer_keys[i, h])
    kbig = kbig.reshape(2, PEER_KEYS * PEER_HEADS, PEER_HEADS * PEER_DQ // 2)
    wq_t = peer_wq[i].reshape(D_MODEL, PEER_HEADS, 2, PEER_DQ // 2)
    wq_t = wq_t.transpose(2, 1, 3, 0).reshape(PEER_HEADS * PEER_DQ, D_MODEL)
    wq_hi, wq_lo = _hi_lo(wq_t)
    kb_hi, kb_lo = _hi_lo(kbig)
    sink_q = sinks[i]
    sink_lanes = jnp.stack([
        jnp.concatenate([jnp.full((HEAD_DIM,), sink_q[j * 2 + g]) for j in range(N_KV_C)])
        for g in range(N_Q_C // N_KV_C)]).reshape(N_Q_C // N_KV_C, 1, LANES)
    return {
        "norm_mix": row(norm_mix[i]),
        "w_in": w_in[i][:, in_perm].astype(BF),
        "conv_w": conv_w[i], "conv_b": row(conv_b[i]),
        "wa_bd": _block_diag(lru_wa[i]), "wx_bd": _block_diag(lru_wx[i]),
        "lru_ba": row(lru_ba[i]), "lru_bx": row(lru_bx[i]), "lru_lambda": row(lru_lambda[i]),
        "sgu_norm": row(sgu_norm[i]),
        "sgu_ws_cat": jnp.concatenate([ws_c[g] for g in range(nh)], axis=1),
        "sgu_bias": jnp.repeat(sgu_b[i].T, HEAD_DIM, axis=1),
        "sgu_mw_s": jnp.repeat(ws_c[:, :nt_s, :nt_s].transpose(1, 2, 0), HEAD_DIM,
                               axis=-1).reshape(nt_s, nt_s, 1, W_GROUP),
        "sgu_mb_s": jnp.repeat(sgu_b[i][:, :nt_s].T, HEAD_DIM, axis=-1).reshape(nt_s, 1, W_GROUP),
        "q_gain": row(jnp.tile(q_norm[i], N_Q_C)), "k_gain": row(jnp.tile(k_norm[i], N_KV_C)),
        "sinks": sink_q, "sink_lanes": sink_lanes,
        "hgrn_lb": row(lower_bound[i]), "hgrn_gn": row(jnp.tile(hgrn_gnorm[i], HGRN_HEADS)),
        "out_norm": row(out_norm[i][mix_perm]),
        "w_out": w_out[i][mix_perm, :].astype(BF),
        "norm_ffn": row(norm_ffn[i]),
        "wq_t_hi": wq_hi, "wq_t_lo": wq_lo, "kbig_hi": kb_hi, "kbig_lo": kb_lo,
        "peer_u": peer_u[i].astype(BF), "peer_v": peer_v[i].astype(BF),
    }


def _state_to_bd_t(s):
    b = s.shape[0]
    st = jnp.zeros((b, HGRN_HEADS, HEAD_DIM, HGRN_HEADS, HEAD_DIM), F32)
    for h in range(HGRN_HEADS):
        st = st.at[:, h, :, h, :].set(jnp.swapaxes(s[:, h], 1, 2))
    return st.reshape(b, W_GROUP, W_GROUP)


def _bd_t_to_state(st):
    b = st.shape[0]
    s5 = st.reshape(b, HGRN_HEADS, HEAD_DIM, HGRN_HEADS, HEAD_DIM)
    return jnp.stack([jnp.swapaxes(s5[:, h, :, h, :], 1, 2) for h in range(HGRN_HEADS)], axis=1)


def _peer_block(rows, h2, x1, mod, p):
    e1t, e2t, gt = _route(h2, p)
    return _peer(rows, h2, e1t.T, e2t.T, gt.T, p["peer_u"], p["peer_v"], x1, mod)


def kernel(x_prompt, x_sample, state_lru_h, state_lru_conv, state_swa_k, state_swa_v, state_hgrn_S,
           c_prompt, c_sample, w_ada, b_ada, norm_mix, w_in, conv_w, conv_b, lru_wa, lru_ba, lru_wx,
           lru_bx, lru_lambda, sgu_norm, sgu_ws, sgu_b, q_norm, k_norm, sinks, hgrn_lb, hgrn_gnorm,
           out_norm, w_out, norm_ffn, peer_wq, peer_keys, peer_u, peer_v):
    bp, tp, _ = x_prompt.shape
    bs, ts, _ = x_sample.shape
    n_s = bs * ts
    win = state_swa_k.shape[2]

    lbp = jax.nn.softmax(hgrn_lb.astype(F32), axis=0)
    lower_bound = jnp.cumsum(lbp, axis=0) - lbp[0]

    c_all = jnp.concatenate([c_prompt, c_sample], axis=0)
    pad_c = (-c_all.shape[0]) % SUBLANES
    mods = _ada(jnp.pad(c_all, ((0, pad_c), (0, 0))), w_ada, b_ada)

    cos_p, sin_p = _rope_tables(jnp.arange(tp, dtype=jnp.int32))
    cos_s, sin_s = _rope_tables(PAST_LEN + jnp.arange(ts, dtype=jnp.int32))
    cos_s, sin_s = cos_s.reshape(ts, 1, LANES), sin_s.reshape(ts, 1, LANES)

    rows_p = _Rows(bp, tp // 512, 512, per_row_mod=False)
    rows_s = _Rows(1, 1, n_s, per_row_mod=True)
    peer_rows_p = _Rows(bp, tp // 512, 512, per_row_mod=False)
    peer_rows_s = _Rows(1, n_s // 512, 512, per_row_mod=True)

    xp = x_prompt.reshape(bp * tp, D_MODEL)
    xs = jnp.swapaxes(x_sample, 0, 1).reshape(n_s, D_MODEL)
    st_p, st_s = [], []
    for i in range(DEPTH):
        p = _layer_params(i, ts, lower_bound, w_in, conv_w, conv_b, lru_wa, lru_ba, lru_wx, lru_bx,
                          lru_lambda, sgu_norm, sgu_ws, sgu_b, q_norm, k_norm, sinks, hgrn_gnorm,
                          out_norm, w_out, norm_ffn, norm_mix, peer_wq, peer_keys, peer_u, peer_v)
        mod_p = mods[i, :bp].reshape(bp, 1, 6 * D_MODEL)
        mod_s = jnp.tile(mods[i, bp:bp + bs], (ts, 1))

        z = _in_proj(rows_p, xp, mod_p, p["norm_mix"], p["w_in"])
        ya, h_fin, tail = _lru_prompt(z, bp, tp, p)
        yb = _sgu_prompt(z, bp, tp, p)
        yc, k_last, v_last = _swa_prompt(z, bp, tp, p, cos_p, sin_p)
        yd, s_fin = _hgrn(z.reshape(bp, tp, D_IN), 3, 4, bp, 512, p)
        yd = yd.reshape(bp * tp, W_GROUP)
        x1, h2 = _out_proj(rows_p, xp, (ya, yb, yc, yd), mod_p, p["out_norm"], p["w_out"], p["norm_ffn"])
        xp = _peer_block(peer_rows_p, h2, x1, mod_p, p)
        st_p.append((h_fin.reshape(bp, W_GROUP), tail[:, SUBLANES - (CONV_W - 1):, :],
                     k_last.reshape(bp, WINDOW, N_KV_C, HEAD_DIM),
                     v_last.reshape(bp, WINDOW, N_KV_C, HEAD_DIM), _bd_t_to_state(s_fin)))

        zs = _in_proj(rows_s, xs, mod_s, p["norm_mix"], p["w_in"])
        ya, h_fin, nbuf, yb, vrows = _ab_sample(
            zs, state_lru_h[:, i], jnp.swapaxes(state_lru_conv[:, i], 0, 1), p, bs, ts)
        kb = state_swa_k[:, i].reshape(bs, win, LANES)
        vb = state_swa_v[:, i].reshape(bs, win, LANES)
        yc3, knew = _swa_sample(zs.reshape(ts, bs, D_IN), cos_s, sin_s, p,
                                jnp.swapaxes(kb, 0, 1), jnp.swapaxes(vb, 0, 1), bs, ts)
        vnew = zs.reshape(ts, bs, D_IN)[:, :, 4 * W_GROUP + 3 * LANES:4 * W_GROUP + 4 * LANES]
        zd = jnp.swapaxes(zs.reshape(ts, bs, D_IN)[:, :, 6 * W_GROUP:], 0, 1)
        zd = jnp.pad(zd, ((0, 0), (0, HG_SUB - ts), (0, 0)))
        yd_pad, s_fin = _hgrn(zd, 0, 1, HG_SAMPLE_SEQS, HG_SUB, p,
                              s0=_state_to_bd_t(state_hgrn_S[:, i]), n_valid=ts)
        yd = jnp.swapaxes(yd_pad[:, :ts], 0, 1).reshape(n_s, W_GROUP)
        x1, h2 = _out_proj(rows_s, xs, (ya, yb, yc3.reshape(n_s, 2 * LANES), yd), mod_s,
                           p["out_norm"], p["w_out"], p["norm_ffn"])
        xs = _peer_block(peer_rows_s, h2, x1, mod_s, p)
        k_win = jnp.concatenate([kb, jnp.swapaxes(knew, 0, 1)], axis=1)[:, -win:]
        v_win = jnp.concatenate([vb, jnp.swapaxes(vnew, 0, 1)], axis=1)[:, -win:]
        st_s.append((h_fin, jnp.swapaxes(nbuf, 0, 1),
                     k_win.reshape(bs, win, N_KV_C, HEAD_DIM), v_win.reshape(bs, win, N_KV_C, HEAD_DIM),
                     _bd_t_to_state(s_fin),
                     jnp.swapaxes(vrows.reshape(ts, bs, W_GROUP), 0, 1)))

    stack = lambda per_layer, j: jnp.stack([s[j] for s in per_layer], axis=1)
    y_p = xp.reshape(bp, tp, D_MODEL)
    y_s = jnp.swapaxes(xs.reshape(ts, bs, D_MODEL), 0, 1)
    return (y_p, y_s,
            stack(st_p, 0), stack(st_p, 1), stack(st_p, 2), stack(st_p, 3), stack(st_p, 4),
            stack(st_s, 0), stack(st_s, 1), stack(st_s, 2), stack(st_s, 3), stack(st_s, 4),
            stack(st_s, 5))
```

```python
import functools

import numpy as np
import jax
import jax.numpy as jnp
from jax import lax
from jax.experimental import pallas as pl
from jax.experimental.pallas import tpu as pltpu

F32 = jnp.float32
BF = jnp.bfloat16

D_MODEL = 1024
DEPTH = 2
PAST_LEN = 16384
HEAD_DIM = 64
W_GROUP = 256
N_GROUPS = 4
CONV_W = 4
LRU_C = 8.0
LRU_BLOCKS = 4
LRU_FLOOR = 1e-12
CHUNK_B = 128
WINDOW = 128
N_Q_C = 4
N_KV_C = 2
ROPE_THETA = 10000.0
NEG_BIG = -1e30
HGRN_HEADS = 4
PEER_HEADS = 8
PEER_KEYS = 128
PEER_DQ = 128
PEER_TOPK = 16
EPS = 1e-6
D_IN = 2560

LANES = 128
SUBLANES = 8
VMEM_LIMIT = 56 * 1024 * 1024

HG_SUB = 16
HG_SUPER = 128
HG_SAMPLE_SEQS = 4


def _cparams(sem):
    return pltpu.CompilerParams(dimension_semantics=sem, vmem_limit_bytes=VMEM_LIMIT)


_NN = (((1,), (0,)), ((), ()))
_NT = (((1,), (1,)), ((), ()))
_TN = (((0,), (0,)), ((), ()))


def _dg(a, b, dims):
    return lax.dot_general(a, b, dims, preferred_element_type=F32)


def _hi_lo(a):
    hi = a.astype(BF)
    lo = (a - hi.astype(F32)).astype(BF)
    return hi, lo


def _mm3(a, b, dims=_NN):
    ah, al = _hi_lo(a)
    bh, bl = _hi_lo(b)
    return (_dg(ah, bl, dims) + _dg(al, bh, dims)) + _dg(ah, bh, dims)


def _split3(a):
    a1 = a.astype(BF)
    r1 = a - a1.astype(F32)
    a2 = r1.astype(BF)
    a3 = (r1 - a2.astype(F32)).astype(BF)
    return a1, a2, a3


def _mm_x01(a, b01, dims=_NN):
    a1, a2, a3 = _split3(a)
    return (_dg(a3, b01, dims) + _dg(a2, b01, dims)) + _dg(a1, b01, dims)


def _mm_x01_2(a, b01, dims=_NN):
    ah, al = _hi_lo(a)
    return _dg(al, b01, dims) + _dg(ah, b01, dims)


def _mm_01x(a01, b, dims=_NN):
    b1, b2, b3 = _split3(b)
    return (_dg(a01, b3, dims) + _dg(a01, b2, dims)) + _dg(a01, b1, dims)


def _sigmoid(x):
    return jax.nn.sigmoid(x)


def _silu(x):
    return x * jax.nn.sigmoid(x)


def _gelu(x):
    return x * (0.5 * (1.0 + jnp.tanh(0.7978845608028654 * (x + 0.044715 * (x * x * x)))))


def _rms(x, w):
    return x * lax.rsqrt(jnp.mean(x * x, axis=-1, keepdims=True) + EPS) * w


def _seg_ones(width, seg):
    r = lax.broadcasted_iota(jnp.int32, (width, width), 0) // seg
    c = lax.broadcasted_iota(jnp.int32, (width, width), 1) // seg
    return jnp.where(r == c, 1.0, 0.0).astype(BF)


def _head_rms(x, gain):
    ms = _mm_x01(x * x, _seg_ones(x.shape[-1], HEAD_DIM)) * (1.0 / HEAD_DIM)
    return x * lax.rsqrt(ms + EPS) * gain


def _rope(x, cos_f, sin_s):
    w = x.shape[-1]
    lane = lax.broadcasted_iota(jnp.int32, x.shape, x.ndim - 1) % HEAD_DIM
    rot = jnp.where(lane < HEAD_DIM // 2,
                    pltpu.roll(x, w - HEAD_DIM // 2, x.ndim - 1),
                    pltpu.roll(x, HEAD_DIM // 2, x.ndim - 1))
    return x * cos_f + rot * sin_s


def _lane_head_mask(width, j):
    lane = lax.broadcasted_iota(jnp.int32, (1, width), 1) // HEAD_DIM
    return jnp.where(lane == j, 1.0, 0.0)


def _ada_body(c_ref, w_ref, b_ref, o_ref):
    c = _silu(c_ref[...])
    o_ref[...] = _mm3(c, w_ref[...]) + b_ref[...]


def _ada(c_all, w_ada, b_ada):
    n = c_all.shape[0]
    cb = 1536
    return pl.pallas_call(
        _ada_body,
        grid=(DEPTH, 6 * D_MODEL // cb),
        in_specs=[
            pl.BlockSpec((n, D_MODEL), lambda l, j: (0, 0)),
            pl.BlockSpec((None, D_MODEL, cb), lambda l, j: (l, 0, j)),
            pl.BlockSpec((None, 1, cb), lambda l, j: (l, 0, j)),
        ],
        out_specs=pl.BlockSpec((None, n, cb), lambda l, j: (l, 0, j)),
        out_shape=jax.ShapeDtypeStruct((DEPTH, n, 6 * D_MODEL), F32),
        compiler_params=_cparams(("arbitrary", "arbitrary")),
        name="ada",
    )(c_all, w_ada, b_ada.reshape(DEPTH, 1, 6 * D_MODEL))


class _Rows:
    def __init__(self, nb, nt, tn, per_row_mod):
        self.nb, self.nt, self.tn, self.per_row_mod = nb, nt, tn, per_row_mod

    def spec(self, width, col=0):
        nt = self.nt
        return pl.BlockSpec((self.tn, width), lambda b, t: (b * nt + t, col))

    def mod_spec(self, j):
        if self.per_row_mod:
            nt = self.nt
            return pl.BlockSpec((self.tn, D_MODEL), lambda b, t: (b * nt + t, j))
        return pl.BlockSpec((None, 1, D_MODEL), lambda b, t: (b, 0, j))

    @property
    def grid(self):
        return (self.nb, self.nt)

    @property
    def rows(self):
        return self.nb * self.nt * self.tn


def _const_spec(shape):
    nd = len(shape)
    return pl.BlockSpec(shape, lambda b, t: (0,) * nd)


def _in_body(x_ref, nw_ref, sh_ref, sc_ref, w_ref, z_ref):
    h = _rms(x_ref[...], nw_ref[...])
    h = h * (1.0 + sc_ref[...]) + sh_ref[...]
    z_ref[...] = jnp.dot(h.astype(BF), w_ref[...], preferred_element_type=F32)


def _in_proj(rows, x, mod, nw, w_bf):
    return pl.pallas_call(
        _in_body,
        grid=rows.grid,
        in_specs=[rows.spec(D_MODEL), _const_spec((1, D_MODEL)),
                  rows.mod_spec(0), rows.mod_spec(1), _const_spec((D_MODEL, D_IN))],
        out_specs=rows.spec(D_IN),
        out_shape=jax.ShapeDtypeStruct((rows.rows, D_IN), F32),
        compiler_params=_cparams(("arbitrary", "arbitrary")),
        name="in_proj",
    )(x, nw, mod, mod, w_bf)


def _out_body(x_ref, ya_ref, yb_ref, yc_ref, yd_ref, on_ref, w_ref, g1_ref, nf_ref,
              sh_ref, sc_ref, x1_ref, h2_ref):
    ys = []
    for g, r in enumerate((ya_ref, yb_ref, yc_ref, yd_ref)):
        ys.append(_rms(r[...], on_ref[:, g * W_GROUP:(g + 1) * W_GROUP]).astype(BF))
    o = jnp.dot(jnp.concatenate(ys, axis=-1), w_ref[...], preferred_element_type=F32)
    x1 = x_ref[...] + g1_ref[...] * o
    x1_ref[...] = x1
    h2 = _rms(x1, nf_ref[...])
    h2_ref[...] = h2 * (1.0 + sc_ref[...]) + sh_ref[...]


def _out_proj(rows, x, ys, mod, on, w_bf, nf):
    sds = jax.ShapeDtypeStruct((rows.rows, D_MODEL), F32)
    return pl.pallas_call(
        _out_body,
        grid=rows.grid,
        in_specs=[rows.spec(D_MODEL)] + [rows.spec(W_GROUP)] * 4
        + [_const_spec((1, D_MODEL)), _const_spec((D_MODEL, D_MODEL)), rows.mod_spec(2),
           _const_spec((1, D_MODEL)), rows.mod_spec(3), rows.mod_spec(4)],
        out_specs=[rows.spec(D_MODEL), rows.spec(D_MODEL)],
        out_shape=[sds, sds],
        compiler_params=_cparams(("arbitrary", "arbitrary")),
        name="out_proj",
    )(x, *ys, on, w_bf, mod, nf, mod, mod)


def _lru_gates(xc, wa, wx, ba, bx, lam):
    r = _sigmoid(_mm3(xc, wa) + ba)
    ig = _sigmoid(_mm3(xc, wx) + bx)
    nl = -lam
    sp = jnp.maximum(nl, 0.0) + jnp.log1p(jnp.exp(-jnp.abs(nl)))
    log_a = (-LRU_C) * sp * r
    a = jnp.exp(log_a)
    x2 = 2.0 * log_a
    em1 = jnp.tanh(0.5 * x2) * (jnp.exp(x2) + 1.0)
    mult = jnp.sqrt(jnp.maximum(-em1, LRU_FLOOR))
    return a, mult * (ig * xc)


def _lru_prompt_body(z_ref, cw_ref, cb_ref, wa_ref, wx_ref, ba_ref, bx_ref, lam_ref,
                     y_ref, hfin_ref, tail_ref, xbuf, hcar, *, tb):
    t = pl.program_id(1)

    @pl.when(t == 0)
    def _():
        xbuf[0:SUBLANES, :] = jnp.zeros((SUBLANES, W_GROUP), F32)
        hcar[...] = jnp.zeros_like(hcar)

    xa = z_ref[:, 0:W_GROUP]
    ga = z_ref[:, W_GROUP:2 * W_GROUP]
    xbuf[SUBLANES:SUBLANES + tb, :] = xa
    xc = cb_ref[...] + cw_ref[3:4, :] * xa
    for k in range(CONV_W - 1):
        xc = xc + cw_ref[k:k + 1, :] * xbuf[pl.ds(SUBLANES - (CONV_W - 1) + k, tb), :]
    xbuf[0:SUBLANES, :] = xa[tb - SUBLANES:tb, :]
    tail_ref[...] = xa[tb - SUBLANES:tb, :]

    a, b = _lru_gates(xc, wa_ref[...], wx_ref[...], ba_ref[...], bx_ref[...], lam_ref[...])
    row = lax.broadcasted_iota(jnp.int32, (tb, W_GROUP), 0)
    d = 1
    while d < tb:
        a_s = pltpu.roll(a, d, 0)
        b_s = pltpu.roll(b, d, 0)
        m = row >= d
        b = jnp.where(m, a * b_s + b, b)
        a = jnp.where(m, a * a_s, a)
        d *= 2
    h = b + a * hcar[0:1, :]
    hl = h[tb - 1:tb, :]
    hcar[...] = jnp.broadcast_to(hl, hcar.shape)
    hfin_ref[...] = hl
    y_ref[...] = h * _gelu(ga)


def _lru_prompt(z, nb, t_len, p, tb=512):
    nt = t_len // tb
    w2 = (1, W_GROUP)
    return pl.pallas_call(
        functools.partial(_lru_prompt_body, tb=tb),
        grid=(nb, nt),
        in_specs=[pl.BlockSpec((tb, 2 * W_GROUP), lambda b, t: (b * nt + t, 0)),
                  _const_spec((CONV_W, W_GROUP)), _const_spec(w2),
                  _const_spec((W_GROUP, W_GROUP)), _const_spec((W_GROUP, W_GROUP)),
                  _const_spec(w2), _const_spec(w2), _const_spec(w2)],
        out_specs=[pl.BlockSpec((tb, W_GROUP), lambda b, t: (b * nt + t, 0)),
                   pl.BlockSpec((None, 1, W_GROUP), lambda b, t: (b, 0, 0)),
                   pl.BlockSpec((None, SUBLANES, W_GROUP), lambda b, t: (b, 0, 0))],
        out_shape=[jax.ShapeDtypeStruct((nb * t_len, W_GROUP), F32),
                   jax.ShapeDtypeStruct((nb, 1, W_GROUP), F32),
                   jax.ShapeDtypeStruct((nb, SUBLANES, W_GROUP), F32)],
        scratch_shapes=[pltpu.VMEM((tb + SUBLANES, W_GROUP), F32),
                        pltpu.VMEM((SUBLANES, W_GROUP), F32)],
        compiler_params=_cparams(("arbitrary", "arbitrary")),
        name="lru_prompt",
    )(z, p["conv_w"], p["conv_b"], p["wa_bd"], p["wx_bd"], p["lru_ba"], p["lru_bx"], p["lru_lambda"])


def _sgu_prompt_body(z_ref, gain_ref, ws_ref, bias_ref, y_ref, *, tb):
    u = _gelu(z_ref[:, 0:W_GROUP])
    v = _rms(_gelu(z_ref[:, W_GROUP:2 * W_GROUP]), gain_ref[...])
    masks = [_lane_head_mask(W_GROUP, g) for g in range(W_GROUP // HEAD_DIM)]
    ws = ws_ref[...]
    bias = bias_ref[...]
    for j in range(tb // CHUNK_B):
        vj = v[j * CHUNK_B:(j + 1) * CHUNK_B, :]
        rhs = jnp.concatenate([vj * m for m in masks], axis=0)
        mix = _mm3(ws, rhs) + bias
        y_ref[j * CHUNK_B:(j + 1) * CHUNK_B, :] = u[j * CHUNK_B:(j + 1) * CHUNK_B, :] * mix


def _sgu_prompt(z, nb, t_len, p, tb=512):
    nt = t_len // tb
    nh = W_GROUP // HEAD_DIM
    return pl.pallas_call(
        functools.partial(_sgu_prompt_body, tb=tb),
        grid=(nb, nt),
        in_specs=[pl.BlockSpec((tb, 2 * W_GROUP), lambda b, t: (b * nt + t, 1)),
                  _const_spec((1, W_GROUP)), _const_spec((CHUNK_B, nh * CHUNK_B)),
                  _const_spec((CHUNK_B, W_GROUP))],
        out_specs=pl.BlockSpec((tb, W_GROUP), lambda b, t: (b * nt + t, 0)),
        out_shape=jax.ShapeDtypeStruct((nb * t_len, W_GROUP), F32),
        compiler_params=_cparams(("arbitrary", "arbitrary")),
        name="sgu_prompt",
    )(z, p["sgu_norm"], p["sgu_ws_cat"], p["sgu_bias"])


def _ab_sample_body(z_ref, h0_ref, buf_ref, cw_ref, cb_ref, wa_ref, wx_ref, ba_ref, bx_ref,
                    lam_ref, gain_ref, mw_ref, mb_ref,
                    ya_ref, hfin_ref, nbuf_ref, yb_ref, vrow_ref, *, nb, nt):
    xa = z_ref[:, 0:W_GROUP]
    ga = z_ref[:, W_GROUP:2 * W_GROUP]
    slabs = [buf_ref[k] for k in range(CONV_W - 1)] + [xa[t * nb:(t + 1) * nb, :] for t in range(nt)]
    xcs = []
    for t in range(nt):
        xc = cb_ref[...]
        for k in range(CONV_W):
            xc = xc + cw_ref[k:k + 1, :] * slabs[t + k]
        xcs.append(xc)
    for k in range(CONV_W - 1):
        nbuf_ref[k] = slabs[nt + k]
    xc = jnp.concatenate(xcs, axis=0)
    a, b = _lru_gates(xc, wa_ref[...], wx_ref[...], ba_ref[...], bx_ref[...], lam_ref[...])
    h = h0_ref[...]
    hs = []
    for t in range(nt):
        h = a[t * nb:(t + 1) * nb, :] * h + b[t * nb:(t + 1) * nb, :]
        hs.append(h)
    hfin_ref[...] = h
    ya_ref[...] = jnp.concatenate(hs, axis=0) * _gelu(ga)

    u = _gelu(z_ref[:, 2 * W_GROUP:3 * W_GROUP])
    v = _rms(_gelu(z_ref[:, 3 * W_GROUP:4 * W_GROUP]), gain_ref[...])
    vrow_ref[...] = v
    for t in range(nt):
        mix = mb_ref[t]
        for s in range(t + 1):
            mix = mix + mw_ref[t, s] * v[s * nb:(s + 1) * nb, :]
        yb_ref[t * nb:(t + 1) * nb, :] = u[t * nb:(t + 1) * nb, :] * mix


def _ab_sample(z, h0, buf, p, nb, nt):
    n = nb * nt
    full = lambda shape: pl.BlockSpec(shape, lambda i: (0,) * len(shape))
    w2 = (1, W_GROUP)
    sd = lambda *s: jax.ShapeDtypeStruct(s, F32)
    return pl.pallas_call(
        functools.partial(_ab_sample_body, nb=nb, nt=nt),
        grid=(1,),
        in_specs=[full((n, 4 * W_GROUP)), full((nb, W_GROUP)), full((CONV_W - 1, nb, W_GROUP)),
                  full((CONV_W, W_GROUP)), full(w2), full((W_GROUP, W_GROUP)),
                  full((W_GROUP, W_GROUP)), full(w2), full(w2), full(w2), full(w2),
                  full((nt, nt, 1, W_GROUP)), full((nt, 1, W_GROUP))],
        out_specs=[full((n, W_GROUP)), full((nb, W_GROUP)), full((CONV_W - 1, nb, W_GROUP)),
                   full((n, W_GROUP)), full((n, W_GROUP))],
        out_shape=[sd(n, W_GROUP), sd(nb, W_GROUP), sd(CONV_W - 1, nb, W_GROUP),
                   sd(n, W_GROUP), sd(n, W_GROUP)],
        compiler_params=_cparams(("arbitrary",)),
        name="ab_sample",
    )(z, h0, buf, p["conv_w"], p["conv_b"], p["wa_bd"], p["wx_bd"], p["lru_ba"], p["lru_bx"],
      p["lru_lambda"], p["sgu_norm"], p["sgu_mw_s"], p["sgu_mb_s"])


def _swa_prompt_body(sink_ref, z_ref, cos_ref, sin_ref, qg_ref, kg_ref,
                     y_ref, klast_ref, vlast_ref, kprev, vprev, *, tb):
    t = pl.program_id(1)

    @pl.when(t == 0)
    def _():
        kprev[...] = jnp.zeros_like(kprev)
        vprev[...] = jnp.zeros_like(vprev)

    cos = cos_ref[...]
    sin = sin_ref[...]
    q = _head_rms(z_ref[:, 0:2 * LANES], qg_ref[...])
    q = _rope(q, jnp.concatenate([cos, cos], axis=-1), jnp.concatenate([sin, sin], axis=-1))
    k = _rope(_head_rms(z_ref[:, 2 * LANES:3 * LANES], kg_ref[...]), cos, sin)
    v = z_ref[:, 3 * LANES:4 * LANES]
    masks = [_lane_head_mask(LANES, j) for j in range(N_KV_C)]
    r = lax.broadcasted_iota(jnp.int32, (WINDOW, 2 * WINDOW), 0)
    c = lax.broadcasted_iota(jnp.int32, (WINDOW, 2 * WINDOW), 1)
    diff = r + WINDOW - c
    band = (diff >= 0) & (diff < WINDOW)
    nsub = tb // WINDOW
    for i in range(nsub):
        sl = slice(i * WINDOW, (i + 1) * WINDOW)
        ki, vi = k[sl, :], v[sl, :]
        kk = jnp.concatenate([kprev[...], ki], axis=0)
        vv = jnp.concatenate([vprev[...], vi], axis=0)
        first = (t * nsub + i) == 0
        ok = band & ((c >= WINDOW) | jnp.logical_not(first))
        outs = []
        for g in range(N_Q_C // N_KV_C):
            qg = q[sl, g * LANES:(g + 1) * LANES]
            og = jnp.zeros((WINDOW, LANES), F32)
            for j in range(N_KV_C):
                s = _mm3(qg * masks[j], kk, _NT) * (HEAD_DIM ** -0.5)
                s = jnp.where(ok, s, NEG_BIG)
                sink = sink_ref[j * (N_Q_C // N_KV_C) + g]
                m = jnp.maximum(jnp.max(s, axis=-1, keepdims=True), sink)
                e = jnp.exp(s - m)
                den = jnp.sum(e, axis=-1, keepdims=True) + jnp.exp(sink - m)
                og = og + _mm3(e / den, vv * masks[j])
            outs.append(og)
        y_ref[sl, :] = jnp.concatenate(outs, axis=-1)
        kprev[...] = ki
        vprev[...] = vi
    klast_ref[...] = k[tb - WINDOW:tb, :]
    vlast_ref[...] = v[tb - WINDOW:tb, :]


def _swa_prompt(z, nb, t_len, p, cos_t, sin_t, tb=512):
    nt = t_len // tb
    sd = lambda *s: jax.ShapeDtypeStruct(s, F32)
    return pl.pallas_call(
        functools.partial(_swa_prompt_body, tb=tb),
        grid=(nb, nt),
        in_specs=[pl.BlockSpec(memory_space=pltpu.SMEM),
                  pl.BlockSpec((tb, 4 * LANES), lambda b, t: (b * nt + t, 2)),
                  pl.BlockSpec((tb, LANES), lambda b, t: (t, 0)),
                  pl.BlockSpec((tb, LANES), lambda b, t: (t, 0)),
                  _const_spec((1, 2 * LANES)), _const_spec((1, LANES))],
        out_specs=[pl.BlockSpec((tb, 2 * LANES), lambda b, t: (b * nt + t, 0)),
                   pl.BlockSpec((None, WINDOW, LANES), lambda b, t: (b, 0, 0)),
                   pl.BlockSpec((None, WINDOW, LANES), lambda b, t: (b, 0, 0))],
        out_shape=[sd(nb * t_len, 2 * LANES), sd(nb, WINDOW, LANES), sd(nb, WINDOW, LANES)],
        scratch_shapes=[pltpu.VMEM((WINDOW, LANES), F32), pltpu.VMEM((WINDOW, LANES), F32)],
        compiler_params=_cparams(("arbitrary", "arbitrary")),
        name="swa_prompt",
    )(p["sinks"], z, cos_t, sin_t, p["q_gain"], p["k_gain"])


def _swa_sample_body(z_ref, cos_ref, sin_ref, qg_ref, kg_ref, sink_ref, kb_ref, vb_ref,
                     y_ref, knew_ref, *, bb, nt):
    seg = _seg_ones(LANES, HEAD_DIM)
    kb = kb_ref[...]
    vb = vb_ref[...]
    nkeys = kb.shape[0]
    kidx = lax.broadcasted_iota(jnp.int32, kb.shape, 0)
    qs, ks, vs = [], [], []
    for t in range(nt):
        zt = z_ref[t]
        cos, sin = cos_ref[t], sin_ref[t]
        q = _head_rms(zt[:, 0:2 * LANES], qg_ref[...])
        qs.append(_rope(q, jnp.concatenate([cos, cos], axis=-1), jnp.concatenate([sin, sin], axis=-1)))
        kt = _rope(_head_rms(zt[:, 2 * LANES:3 * LANES], kg_ref[...]), cos, sin)
        ks.append(kt)
        vs.append(zt[:, 3 * LANES:4 * LANES])
        knew_ref[t] = kt
    scale = HEAD_DIM ** -0.5
    for t in range(nt):
        outs = []
        for g in range(N_Q_C // N_KV_C):
            qtg = qs[t][:, g * LANES:(g + 1) * LANES]
            prod = (qtg[None, :, :] * kb).reshape(nkeys * bb, LANES)
            sb = (_mm_x01(prod, seg) * scale).reshape(nkeys, bb, LANES)
            sb = jnp.where(kidx >= t + 1 + (nkeys - WINDOW), sb, NEG_BIG)
            sn = [_mm_x01(qtg * ks[s], seg) * scale for s in range(t + 1)]
            sink = sink_ref[g]
            m = jnp.maximum(jnp.max(sb, axis=0), sink)
            for x in sn:
                m = jnp.maximum(m, x)
            eb = jnp.exp(sb - m[None, :, :])
            den = jnp.sum(eb, axis=0) + jnp.exp(sink - m)
            num = jnp.sum(eb * vb, axis=0)
            for s, x in enumerate(sn):
                en = jnp.exp(x - m)
                den = den + en
                num = num + en * vs[s]
            outs.append(num / den)
        y_ref[t] = jnp.concatenate(outs, axis=-1)


def _swa_sample(z3, cos_s, sin_s, p, kb_t, vb_t, nb, nt, bb=16):
    sd = lambda *s: jax.ShapeDtypeStruct(s, F32)
    nk = kb_t.shape[0]
    c3 = lambda shape: pl.BlockSpec(shape, lambda i: (0,) * len(shape))
    return pl.pallas_call(
        functools.partial(_swa_sample_body, bb=bb, nt=nt),
        grid=(nb // bb,),
        in_specs=[pl.BlockSpec((nt, bb, 4 * LANES), lambda i: (0, i, 2)),
                  c3((nt, 1, LANES)), c3((nt, 1, LANES)), c3((1, 2 * LANES)), c3((1, LANES)),
                  c3((N_Q_C // N_KV_C, 1, LANES)),
                  pl.BlockSpec((nk, bb, LANES), lambda i: (0, i, 0)),
                  pl.BlockSpec((nk, bb, LANES), lambda i: (0, i, 0))],
        out_specs=[pl.BlockSpec((nt, bb, 2 * LANES), lambda i: (0, i, 0)),
                   pl.BlockSpec((nt, bb, LANES), lambda i: (0, i, 0))],
        out_shape=[sd(nt, nb, 2 * LANES), sd(nt, nb, LANES)],
        compiler_params=_cparams(("arbitrary",)),
        name="swa_sample",
    )(z3, cos_s, sin_s, p["q_gain"], p["k_gain"], p["sink_lanes"], kb_t, vb_t)


def _hgrn_body(*refs, rows, nq, has_init, n_valid):
    if has_init:
        (zqf_ref, zig_ref, lb_ref, gn_ref, s0_ref, y_ref, sfin_ref,
         st, qq_s, qe_s, kk_s, kd_s, bc_s, bl_s, v_s, o_s) = refs
    else:
        (zqf_ref, zig_ref, lb_ref, gn_ref, y_ref, sfin_ref,
         st, qq_s, qe_s, kk_s, kd_s, bc_s, bl_s, v_s, o_s) = refs
        s0_ref = None
    t = pl.program_id(1)

    @pl.when(t == 0)
    def _():
        if has_init:
            st[...] = s0_ref[...]
        else:
            st[...] = jnp.zeros_like(st)

    sup = HG_SUPER
    n_super = max(rows // sup, 1)
    real = min(rows, sup)
    lb = lb_ref[...]
    seg_heads = _seg_ones(W_GROUP, HEAD_DIM)
    bd_mask = seg_heads.astype(F32)
    ri = lax.broadcasted_iota(jnp.int32, (sup, sup), 0)
    ci = lax.broadcasted_iota(jnp.int32, (sup, sup), 1)
    same = (ri // HG_SUB) == (ci // HG_SUB)
    tri_bd = jnp.where(same & (ci <= ri), 1.0, 0.0).astype(BF)
    ones_bd = jnp.where(same, 1.0, 0.0).astype(BF)
    pr = lax.broadcasted_iota(jnp.int32, (HG_SUB * HG_SUB, W_GROUP), 0)
    causal = jnp.where((pr % HG_SUB) >= (pr // HG_SUB), 1.0, 0.0)

    def pad(x):
        if real == sup:
            return x
        return jnp.concatenate([x, jnp.zeros((sup - real, x.shape[1]), F32)], axis=0)

    for sb in range(n_super):
        rs = slice(sb * sup, sb * sup + real)
        for q in range(nq):
            zf = zqf_ref[q, rs, W_GROUP:2 * W_GROUP]
            f = lb + (1.0 - lb) * _sigmoid(zf)
            logf = jnp.log(jnp.maximum(f, 1e-30))
            kk = (1.0 - lb) * _sigmoid(-zf)
            if n_valid < real:
                valid = lax.broadcasted_iota(jnp.int32, (real, W_GROUP), 0) < n_valid
                logf = jnp.where(valid, logf, 0.0)
                kk = jnp.where(valid, kk, 0.0)
            logf, kk = pad(logf), pad(kk)
            qq = pad(_silu(zqf_ref[q, rs, 0:W_GROUP]))
            v = pad(zig_ref[q, rs, 0:W_GROUP])
            bc = _mm_01x(tri_bd, logf)
            bl = _mm_01x(ones_bd, logf)
            qq_s[q] = qq
            kk_s[q] = kk
            bc_s[q] = bc
            bl_s[q] = bl
            qe_s[q] = qq * jnp.exp(bc)
            kd_s[q] = kk * jnp.exp(bl - bc)
            v_s[q] = v

        n_sub = -(-real // HG_SUB)

        def sub(c, carry):
            r0 = pl.multiple_of(c * HG_SUB, HG_SUB)
            rsub = pl.ds(r0, HG_SUB)
            pms = []
            for q in range(nq):
                bc_c = bc_s[q, rsub, :]
                qq_c = qq_s[q, rsub, :]
                kk_c = kk_s[q, rsub, :]
                parts = []
                for s in range(HG_SUB):
                    e = jnp.exp(jnp.minimum(bc_c - bc_c[s:s + 1, :], 0.0))
                    parts.append(qq_c * e * kk_c[s:s + 1, :])
                pms.append(jnp.concatenate(parts, axis=0) * causal)
            atts = [_mm_x01_2(pm, seg_heads) for pm in pms]
            inter = [_mm3(qe_s[q, rsub, :], st[q], _NT) for q in range(nq)]
            upd = [_mm3(v_s[q, rsub, :], kd_s[q, rsub, :], _TN) for q in range(nq)]
            for q in range(nq):
                v_c = v_s[q, rsub, :]
                o_c = inter[q]
                for s in range(HG_SUB):
                    o_c = o_c + atts[q][s * HG_SUB:(s + 1) * HG_SUB, :] * v_c[s:s + 1, :]
                o_s[q, pl.ds(pl.multiple_of(sb * sup + r0, HG_SUB), HG_SUB), :] = o_c
                decay = jnp.exp(bl_s[q, pl.ds(r0, 1), :])
                st[q] = st[q] * decay + upd[q] * bd_mask
            return carry

        lax.fori_loop(0, n_sub, sub, 0)

    for q in range(nq):
        o = o_s[q, 0:rows, :]
        zg = zig_ref[q, :, W_GROUP:2 * W_GROUP]
        y_ref[q] = _head_rms(o, gn_ref[...]) * _silu(zg)
    sfin_ref[...] = st[...]


def _hgrn(z3, qf_col, ig_col, nq, rows, p, s0=None, n_valid=None):
    nseq, t_len, _ = z3.shape
    nt = t_len // rows
    n_valid = rows if n_valid is None else n_valid
    sd = lambda *s: jax.ShapeDtypeStruct(s, F32)
    in_specs = [pl.BlockSpec((nq, rows, 2 * W_GROUP), lambda b, t: (b, t, qf_col)),
                pl.BlockSpec((nq, rows, 2 * W_GROUP), lambda b, t: (b, t, ig_col)),
                _const_spec((1, W_GROUP)), _const_spec((1, W_GROUP))]
    args = [z3, z3, p["hgrn_lb"], p["hgrn_gn"]]
    if s0 is not None:
        in_specs.append(pl.BlockSpec((nq, W_GROUP, W_GROUP), lambda b, t: (b, 0, 0)))
        args.append(s0)
    scr_rows = max(rows, HG_SUPER)
    sup_shape = pltpu.VMEM((nq, HG_SUPER, W_GROUP), F32)
    return pl.pallas_call(
        functools.partial(_hgrn_body, rows=rows, nq=nq, has_init=s0 is not None, n_valid=n_valid),
        grid=(nseq // nq, nt),
        in_specs=in_specs,
        out_specs=[pl.BlockSpec((nq, rows, W_GROUP), lambda b, t: (b, t, 0)),
                   pl.BlockSpec((nq, W_GROUP, W_GROUP), lambda b, t: (b, 0, 0))],
        out_shape=[sd(nseq, t_len, W_GROUP), sd(nseq, W_GROUP, W_GROUP)],
        scratch_shapes=[pltpu.VMEM((nq, W_GROUP, W_GROUP), F32)] + [sup_shape] * 7
        + [pltpu.VMEM((nq, scr_rows, W_GROUP), F32)],
        compiler_params=_cparams(("arbitrary", "arbitrary")),
        name="hgrn",
    )(*args)


def _oddeven_merge_sort(n):
    pairs = []

    def merge(lo, m, r):
        step = r * 2
        if step < m:
            merge(lo, m, step)
            merge(lo + r, m, step)
            for i in range(lo + r, lo + m - r, step):
                pairs.append((i, i + r))
        else:
            pairs.append((lo, lo + r))

    def sort(lo, m):
        if m > 1:
            h = m // 2
            sort(lo, h)
            sort(lo + h, h)
            merge(lo, m, 1)

    sort(0, n)
    return pairs


_NET16 = _oddeven_merge_sort(PEER_TOPK)
_CELLS = [(a, b) for a in range(PEER_TOPK) for b in range(PEER_TOPK)
          if (a + 1) * (b + 1) <= PEER_TOPK]


def _beats(va, ia, vb, ib):
    return (va > vb) | ((va == vb) & (ia < ib))


def _cmpx(items, i, j):
    (va, ia), (vb, ib) = items[i], items[j]
    gt = _beats(va, ia, vb, ib)
    items[i] = (jnp.where(gt, va, vb), jnp.where(gt, ia, ib))
    items[j] = (jnp.where(gt, vb, va), jnp.where(gt, ib, ia))


def _merge_top(a_items, b_items):
    n = len(a_items)
    items = []
    for i in range(n):
        (va, ia), (vb, ib) = a_items[i], b_items[n - 1 - i]
        gt = _beats(va, ia, vb, ib)
        items.append((jnp.where(gt, va, vb), jnp.where(gt, ia, ib)))
    d = n // 2
    while d >= 1:
        for i in range(n):
            if not (i & d):
                _cmpx(items, i, i + d)
        d //= 2
    return items


def _top16_sorted(load_key, n_keys, shape):
    stack = []
    for g in range(n_keys // PEER_TOPK):
        items = [(load_key(g * PEER_TOPK + j), jnp.full(shape, float(g * PEER_TOPK + j), F32))
                 for j in range(PEER_TOPK)]
        for i, j in _NET16:
            _cmpx(items, i, j)
        level = 0
        while stack and stack[-1][0] == level:
            _, other = stack.pop()
            items = _merge_top(other, items)
            level += 1
        stack.append((level, items))
    assert len(stack) == 1
    return stack[0][1]


def _route_body(h_ref, wqh_ref, wql_ref, kbh_ref, kbl_ref, e1_ref, e2_ref, g_ref,
                st_s, top_s, *, tn):
    hh, hl = _hi_lo(h_ref[...])
    wqh, wql = wqh_ref[...], wql_ref[...]
    qt = (_dg(wqh, hl, _NT) + _dg(wql, hh, _NT)) + _dg(wqh, hh, _NT)
    half = PEER_HEADS * PEER_DQ // 2
    for p in range(2):
        qp = qt[p * half:(p + 1) * half, :]
        qh, ql = _hi_lo(qp)
        kh, kl = kbh_ref[p], kbl_ref[p]
        st_s[p] = (_dg(kh, ql, _NN) + _dg(kl, qh, _NN)) + _dg(kh, qh, _NN)

    shape = (PEER_HEADS, LANES)

    def column(col, carry):
        c0 = pl.multiple_of(col * LANES, LANES)
        cs = pl.ds(c0, LANES)
        for p in range(2):
            top = _top16_sorted(lambda k: st_s[p, k * PEER_HEADS:(k + 1) * PEER_HEADS, cs],
                                PEER_KEYS, shape)
            for a, (va, ia) in enumerate(top):
                top_s[p, 0, a] = va
                top_s[p, 1, a] = ia
        v1 = [top_s[0, 0, a] for a in range(PEER_TOPK)]
        v2 = [top_s[1, 0, b] for b in range(PEER_TOPK)]
        cval = {cell: v1[cell[0]] + v2[cell[1]] for cell in _CELLS}
        rank_static = {cell: float((cell[0] + 1) * (cell[1] + 1) - 1) for cell in _CELLS}
        dyn = {cell: None for cell in _CELLS}
        for xi, x in enumerate(_CELLS):
            for y in _CELLS[xi + 1:]:
                if x[0] < y[0] and x[1] > y[1]:
                    xw = jnp.where(cval[x] >= cval[y], 1.0, 0.0)
                    dyn[y] = xw if dyn[y] is None else dyn[y] + xw
                    lose = 1.0 - xw
                    dyn[x] = lose if dyn[x] is None else dyn[x] + lose
        rank = {cell: (rank_static[cell] if dyn[cell] is None else dyn[cell] + rank_static[cell])
                for cell in _CELLS}
        c00 = cval[(0, 0)]
        ex = {}
        zsum = None
        for cell in _CELLS:
            if dyn[cell] is None:
                w = jnp.exp(cval[cell] - c00)
            else:
                w = jnp.where(rank[cell] < PEER_TOPK, jnp.exp(cval[cell] - c00), 0.0)
            ex[cell] = w
            zsum = w if zsum is None else zsum + w
        inv = 1.0 / zsum
        for k in range(PEER_TOPK):
            e1 = jnp.zeros(shape, F32)
            e2 = jnp.zeros(shape, F32)
            gk = jnp.zeros(shape, F32)
            for cell in _CELLS:
                lo = (cell[0] + 1) * (cell[1] + 1) - 1
                if lo > k:
                    continue
                if dyn[cell] is None:
                    if lo != k:
                        continue
                    e1, e2, gk = top_s[0, 1, cell[0]], top_s[1, 1, cell[1]], ex[cell]
                    continue
                hit = rank[cell] == float(k)
                e1 = jnp.where(hit, top_s[0, 1, cell[0]], e1)
                e2 = jnp.where(hit, top_s[1, 1, cell[1]], e2)
                gk = jnp.where(hit, ex[cell], gk)
            rows = slice(k * PEER_HEADS, (k + 1) * PEER_HEADS)
            e1_ref[rows, cs] = e1
            e2_ref[rows, cs] = e2
            g_ref[rows, cs] = gk * inv
        return carry

    lax.fori_loop(0, tn // LANES, column, 0)


def _route(h2, p, tn=512):
    n = h2.shape[0]
    nslots = PEER_TOPK * PEER_HEADS
    half = PEER_HEADS * PEER_DQ // 2
    nk = PEER_KEYS * PEER_HEADS
    sd = jax.ShapeDtypeStruct((nslots, n), F32)
    c1 = lambda shape: pl.BlockSpec(shape, lambda i: (0,) * len(shape))
    return pl.pallas_call(
        functools.partial(_route_body, tn=tn),
        grid=(n // tn,),
        in_specs=[pl.BlockSpec((tn, D_MODEL), lambda i: (i, 0)),
                  c1((2 * half, D_MODEL)), c1((2 * half, D_MODEL)),
                  c1((2, nk, half)), c1((2, nk, half))],
        out_specs=[pl.BlockSpec((nslots, tn), lambda i: (0, i))] * 3,
        out_shape=[sd, sd, sd],
        scratch_shapes=[pltpu.VMEM((2, nk, tn), F32),
                        pltpu.VMEM((2, 2, PEER_TOPK, PEER_HEADS, LANES), F32)],
        compiler_params=_cparams(("arbitrary",)),
        name="peer_route",
    )(h2, p["wq_t_hi"], p["wq_t_lo"], p["kbig_hi"], p["kbig_lo"])


PEER_E1_STEP = 8
PEER_STEP = PEER_E1_STEP * PEER_KEYS
PEER_BUILD_UNROLL = 2 * SUBLANES
PEER_PAIRS = PEER_KEYS // 2
HI16 = 0xFFFF0000


def _gate_pitch(tn):
    assert (tn // SUBLANES) % 2 == 0
    return tn + SUBLANES


def _peer_body(h_ref, e1_ref, e2_ref, g_ref, u_ref, v_ref, x1_ref, g2_ref, o_ref,
               gt_s, xb_s, acc_s, *, tn):
    cp = pl.program_id(2)
    pitch = _gate_pitch(tn)

    @pl.when(cp == 0)
    def _():
        xb_s[...] = h_ref[...].astype(BF)
        acc_s[...] = jnp.zeros_like(acc_s)
        r = lax.broadcasted_iota(jnp.int32, (PEER_KEYS, LANES), 0)
        sub1 = jnp.where(r < PEER_PAIRS, 2 * r, 2 * (r - PEER_PAIRS) + 1).astype(F32)
        sub2 = r.astype(F32)

        def build(nb, carry):
            r0 = pl.multiple_of(nb * PEER_BUILD_UNROLL, PEER_BUILD_UNROLL)
            e1s = e1_ref[pl.ds(r0, PEER_BUILD_UNROLL), :]
            e2s = e2_ref[pl.ds(r0, PEER_BUILD_UNROLL), :]
            gs = g_ref[pl.ds(r0, PEER_BUILD_UNROLL), :]
            for j in range(PEER_BUILD_UNROLL):
                pt = jnp.where(sub1 == e1s[j:j + 1, :], 1.0, 0.0).astype(BF)
                qg = jnp.where(sub2 == e2s[j:j + 1, :], gs[j:j + 1, :], 0.0).astype(BF)
                gn = _dg(pt, qg, _NT)
                even = lax.bitcast_convert_type(gn[0:PEER_PAIRS, :], jnp.uint32)
                odd = lax.bitcast_convert_type(gn[PEER_PAIRS:PEER_KEYS, :], jnp.uint32)
                gt_s[pl.ds(r0 + j, PEER_PAIRS, stride=pitch), :] = (odd & jnp.uint32(HI16)) | (even >> 16)
            return carry

        lax.fori_loop(0, tn // PEER_BUILD_UNROLL, build, 0)

    a = _gelu(_dg(xb_s[...], u_ref[...], _NT))
    gates = []
    for j in range(PEER_E1_STEP // 2):
        first = pl.multiple_of(((PEER_E1_STEP // 2) * cp + j) * pitch, SUBLANES)
        w = gt_s[pl.ds(first, tn), :]
        gates.append(lax.bitcast_convert_type(w << 16, F32))
        gates.append(lax.bitcast_convert_type(w & jnp.uint32(HI16), F32))
    ga = (a * jnp.concatenate(gates, axis=-1)).astype(BF)
    acc_s[...] += jnp.dot(ga, v_ref[...], preferred_element_type=F32)

    @pl.when(cp == pl.num_programs(2) - 1)
    def _():
        o_ref[...] = x1_ref[...] + g2_ref[...] * acc_s[...]


def _peer(rows, h2, e1, e2, g, u_bf, v_bf, x1, mod):
    tn = rows.tn
    nt = rows.nt
    ncp = PEER_KEYS * PEER_KEYS // PEER_STEP
    tok = lambda width: pl.BlockSpec((tn, width), lambda b, t, c: (b * nt + t, 0))
    if rows.per_row_mod:
        g2_spec = pl.BlockSpec((tn, D_MODEL), lambda b, t, c: (b * nt + t, 5))
    else:
        g2_spec = pl.BlockSpec((None, 1, D_MODEL), lambda b, t, c: (b, 0, 5))
    return pl.pallas_call(
        functools.partial(_peer_body, tn=tn),
        grid=(rows.nb, rows.nt, ncp),
        in_specs=[tok(D_MODEL), tok(LANES), tok(LANES), tok(LANES),
                  pl.BlockSpec((PEER_STEP, D_MODEL), lambda b, t, c: (c, 0)),
                  pl.BlockSpec((PEER_STEP, D_MODEL), lambda b, t, c: (c, 0)),
                  tok(D_MODEL), g2_spec],
        out_specs=tok(D_MODEL),
        out_shape=jax.ShapeDtypeStruct((rows.rows, D_MODEL), F32),
        scratch_shapes=[pltpu.VMEM((_gate_pitch(tn) * PEER_PAIRS, LANES), jnp.uint32),
                        pltpu.VMEM((tn, D_MODEL), BF),
                        pltpu.VMEM((tn, D_MODEL), F32)],
        compiler_params=_cparams(("arbitrary", "arbitrary", "arbitrary")),
        name="peer_experts",
    )(h2, e1, e2, g, u_bf, v_bf, x1, mod)


_Q_HEAD_ORDER = (0, 2, 1, 3)


def _q_perm():
    return np.concatenate([np.arange(h * HEAD_DIM, (h + 1) * HEAD_DIM) for h in _Q_HEAD_ORDER])


def _block_diag(w):
    nblk, bi, bo = w.shape
    eye = jnp.eye(nblk, dtype=w.dtype)
    return (w[:, :, None, :] * eye[:, None, :, None]).reshape(nblk * bi, nblk * bo)


def _rope_tables(pos):
    half = HEAD_DIM // 2
    freqs = ROPE_THETA ** (-jnp.arange(half, dtype=F32) / half)
    ang = pos.astype(F32)[:, None] * freqs[None, :]
    cos, sin = jnp.cos(ang), jnp.sin(ang)
    cos_h = jnp.concatenate([cos, cos], axis=-1)
    sin_h = jnp.concatenate([-sin, sin], axis=-1)
    reps = LANES // HEAD_DIM
    return jnp.tile(cos_h, (1, reps)), jnp.tile(sin_h, (1, reps))


def _layer_params(i, nt_s, lower_bound, w_in, conv_w, conv_b, lru_wa, lru_ba, lru_wx, lru_bx,
                  lru_lambda, sgu_norm, sgu_ws, sgu_b, q_norm, k_norm, sinks, hgrn_gnorm,
                  out_norm, w_out, norm_ffn, norm_mix, peer_wq, peer_keys, peer_u, peer_v):
    row = lambda v: v.reshape(1, -1)
    qperm = _q_perm()
    q0 = 4 * W_GROUP
    in_perm = np.arange(D_IN)
    in_perm[q0:q0 + W_GROUP] = q0 + qperm
    c0 = 2 * W_GROUP
    mix_perm = np.arange(N_GROUPS * W_GROUP)
    mix_perm[c0:c0 + W_GROUP] = c0 + qperm
    nh = W_GROUP // HEAD_DIM
    causal = jnp.tril(jnp.ones((CHUNK_B, CHUNK_B), F32))
    ws_c = sgu_ws[i] * causal
    keys_pkh = peer_keys[i].transpose(1, 2, 0, 3)
    kbig = keys_pkh[:, :, :, None, :] * jnp.eye(PEER_HEADS, dtype=F32)[None, None, :, :, None]
    kbig = kbig.reshape(2, PEER_KEYS * PEER_HEADS, PEER_HEADS * PEER_DQ // 2)
    wq_t = peer_wq[i].reshape(D_MODEL, PEER_HEADS, 2, PEER_DQ // 2)
    wq_t = wq_t.transpose(2, 1, 3, 0).reshape(PEER_HEADS * PEER_DQ, D_MODEL)
    wq_hi, wq_lo = _hi_lo(wq_t)
    kb_hi, kb_lo = _hi_lo(kbig)
    sink_q = sinks[i]
    sink_lanes = jnp.stack([
        jnp.concatenate([jnp.full((HEAD_DIM,), sink_q[j * 2 + g]) for j in range(N_KV_C)])
        for g in range(N_Q_C // N_KV_C)]).reshape(N_Q_C // N_KV_C, 1, LANES)
    return {
        "norm_mix": row(norm_mix[i]),
        "w_in": w_in[i][:, in_perm].astype(BF),
        "conv_w": conv_w[i], "conv_b": row(conv_b[i]),
        "wa_bd": _block_diag(lru_wa[i]), "wx_bd": _block_diag(lru_wx[i]),
        "lru_ba": row(lru_ba[i]), "lru_bx": row(lru_bx[i]), "lru_lambda": row(lru_lambda[i]),
        "sgu_norm": row(sgu_norm[i]),
        "sgu_ws_cat": jnp.concatenate([ws_c[g] for g in range(nh)], axis=1),
        "sgu_bias": jnp.repeat(sgu_b[i].T, HEAD_DIM, axis=1),
        "sgu_mw_s": jnp.repeat(ws_c[:, :nt_s, :nt_s].transpose(1, 2, 0), HEAD_DIM,
                               axis=-1).reshape(nt_s, nt_s, 1, W_GROUP),
        "sgu_mb_s": jnp.repeat(sgu_b[i][:, :nt_s].T, HEAD_DIM, axis=-1).reshape(nt_s, 1, W_GROUP),
        "q_gain": row(jnp.tile(q_norm[i], N_Q_C)), "k_gain": row(jnp.tile(k_norm[i], N_KV_C)),
        "sinks": sink_q, "sink_lanes": sink_lanes,
        "hgrn_lb": row(lower_bound[i]), "hgrn_gn": row(jnp.tile(hgrn_gnorm[i], HGRN_HEADS)),
        "out_norm": row(out_norm[i][mix_perm]),
        "w_out": w_out[i][mix_perm, :].astype(BF),
        "norm_ffn": row(norm_ffn[i]),
        "wq_t_hi": wq_hi, "wq_t_lo": wq_lo, "kbig_hi": kb_hi, "kbig_lo": kb_lo,
        "peer_u": peer_u[i].astype(BF), "peer_v": peer_v[i].astype(BF),
    }


def _state_to_bd_t(s):
    b = s.shape[0]
    eye = jnp.eye(HGRN_HEADS, dtype=F32)
    st = jnp.swapaxes(s, 2, 3)[:, :, :, None, :] * eye[None, :, None, :, None]
    return st.reshape(b, W_GROUP, W_GROUP)


def _bd_t_to_state(st):
    b = st.shape[0]
    s5 = st.reshape(b, HGRN_HEADS, HEAD_DIM, HGRN_HEADS, HEAD_DIM)
    diag = jnp.diagonal(s5, axis1=1, axis2=3)
    return diag.transpose(0, 3, 2, 1)


def _peer_block(rows, h2, x1, mod, p):
    e1t, e2t, gt = _route(h2, p)
    return _peer(rows, h2, e1t.T, e2t.T, gt.T, p["peer_u"], p["peer_v"], x1, mod)


def kernel(x_prompt, x_sample, state_lru_h, state_lru_conv, state_swa_k, state_swa_v, state_hgrn_S,
           c_prompt, c_sample, w_ada, b_ada, norm_mix, w_in, conv_w, conv_b, lru_wa, lru_ba, lru_wx,
           lru_bx, lru_lambda, sgu_norm, sgu_ws, sgu_b, q_norm, k_norm, sinks, hgrn_lb, hgrn_gnorm,
           out_norm, w_out, norm_ffn, peer_wq, peer_keys, peer_u, peer_v):
    bp, tp, _ = x_prompt.shape
    bs, ts, _ = x_sample.shape
    n_s = bs * ts
    win = state_swa_k.shape[2]

    lbp = jax.nn.softmax(hgrn_lb.astype(F32), axis=0)
    lower_bound = jnp.cumsum(lbp, axis=0) - lbp[0]

    c_all = jnp.concatenate([c_prompt, c_sample], axis=0)
    pad_c = (-c_all.shape[0]) % SUBLANES
    mods = _ada(jnp.pad(c_all, ((0, pad_c), (0, 0))), w_ada, b_ada)

    cos_p, sin_p = _rope_tables(jnp.arange(tp, dtype=jnp.int32))
    cos_s, sin_s = _rope_tables(PAST_LEN + jnp.arange(ts, dtype=jnp.int32))
    cos_s, sin_s = cos_s.reshape(ts, 1, LANES), sin_s.reshape(ts, 1, LANES)

    rows_p = _Rows(bp, tp // 512, 512, per_row_mod=False)
    rows_s = _Rows(1, 1, n_s, per_row_mod=True)
    peer_rows_p = _Rows(bp, tp // 512, 512, per_row_mod=False)
    peer_rows_s = _Rows(1, n_s // 512, 512, per_row_mod=True)

    xp = x_prompt.reshape(bp * tp, D_MODEL)
    xs = jnp.swapaxes(x_sample, 0, 1).reshape(n_s, D_MODEL)
    st_p, st_s = [], []
    for i in range(DEPTH):
        p = _layer_params(i, ts, lower_bound, w_in, conv_w, conv_b, lru_wa, lru_ba, lru_wx, lru_bx,
                          lru_lambda, sgu_norm, sgu_ws, sgu_b, q_norm, k_norm, sinks, hgrn_gnorm,
                          out_norm, w_out, norm_ffn, norm_mix, peer_wq, peer_keys, peer_u, peer_v)
        mod_p = mods[i, :bp].reshape(bp, 1, 6 * D_MODEL)
        mod_s = jnp.tile(mods[i, bp:bp + bs], (ts, 1))

        z = _in_proj(rows_p, xp, mod_p, p["norm_mix"], p["w_in"])
        ya, h_fin, tail = _lru_prompt(z, bp, tp, p)
        yb = _sgu_prompt(z, bp, tp, p)
        yc, k_last, v_last = _swa_prompt(z, bp, tp, p, cos_p, sin_p)
        yd, s_fin = _hgrn(z.reshape(bp, tp, D_IN), 3, 4, bp, 512, p)
        yd = yd.reshape(bp * tp, W_GROUP)
        x1, h2 = _out_proj(rows_p, xp, (ya, yb, yc, yd), mod_p, p["out_norm"], p["w_out"], p["norm_ffn"])
        xp = _peer_block(peer_rows_p, h2, x1, mod_p, p)
        st_p.append((h_fin.reshape(bp, W_GROUP), tail[:, SUBLANES - (CONV_W - 1):, :],
                     k_last.reshape(bp, WINDOW, N_KV_C, HEAD_DIM),
                     v_last.reshape(bp, WINDOW, N_KV_C, HEAD_DIM), _bd_t_to_state(s_fin)))

        zs = _in_proj(rows_s, xs, mod_s, p["norm_mix"], p["w_in"])
        ya, h_fin, nbuf, yb, vrows = _ab_sample(
            zs, state_lru_h[:, i], jnp.swapaxes(state_lru_conv[:, i], 0, 1), p, bs, ts)
        kb = state_swa_k[:, i].reshape(bs, win, LANES)
        vb = state_swa_v[:, i].reshape(bs, win, LANES)
        yc3, knew = _swa_sample(zs.reshape(ts, bs, D_IN), cos_s, sin_s, p,
                                jnp.swapaxes(kb, 0, 1), jnp.swapaxes(vb, 0, 1), bs, ts)
        vnew = zs.reshape(ts, bs, D_IN)[:, :, 4 * W_GROUP + 3 * LANES:4 * W_GROUP + 4 * LANES]
        zd = jnp.swapaxes(zs.reshape(ts, bs, D_IN)[:, :, 6 * W_GROUP:], 0, 1)
        zd = jnp.pad(zd, ((0, 0), (0, HG_SUB - ts), (0, 0)))
        yd_pad, s_fin = _hgrn(zd, 0, 1, HG_SAMPLE_SEQS, HG_SUB, p,
                              s0=_state_to_bd_t(state_hgrn_S[:, i]), n_valid=ts)
        yd = jnp.swapaxes(yd_pad[:, :ts], 0, 1).reshape(n_s, W_GROUP)
        x1, h2 = _out_proj(rows_s, xs, (ya, yb, yc3.reshape(n_s, 2 * LANES), yd), mod_s,
                           p["out_norm"], p["w_out"], p["norm_ffn"])
        xs = _peer_block(peer_rows_s, h2, x1, mod_s, p)
        k_win = jnp.concatenate([kb, jnp.swapaxes(knew, 0, 1)], axis=1)[:, -win:]
        v_win = jnp.concatenate([vb, jnp.swapaxes(vnew, 0, 1)], axis=1)[:, -win:]
        st_s.append((h_fin, jnp.swapaxes(nbuf, 0, 1),
                     k_win.reshape(bs, win, N_KV_C, HEAD_DIM), v_win.reshape(bs, win, N_KV_C, HEAD_DIM),
                     _bd_t_to_state(s_fin),
                     jnp.swapaxes(vrows.reshape(ts, bs, W_GROUP), 0, 1)))

    stack = lambda per_layer, j: jnp.stack([s[j] for s in per_layer], axis=1)
    y_p = xp.reshape(bp, tp, D_MODEL)
    y_s = jnp.swapaxes(xs.reshape(ts, bs, D_MODEL), 0, 1)
    return (y_p, y_s,
            stack(st_p, 0), stack(st_p, 1), stack(st_p, 2), stack(st_p, 3), stack(st_p, 4),
            stack(st_s, 0), stack(st_s, 1), stack(st_s, 2), stack(st_s, 3), stack(st_s, 4),
            stack(st_s, 5))
```

```python
import functools

import numpy as np
import jax
import jax.numpy as jnp
from jax import lax
from jax.experimental import pallas as pl
from jax.experimental.pallas import tpu as pltpu

F32 = jnp.float32
BF = jnp.bfloat16

D_MODEL = 1024
DEPTH = 2
PAST_LEN = 16384
HEAD_DIM = 64
W_GROUP = 256
N_GROUPS = 4
CONV_W = 4
LRU_C = 8.0
LRU_BLOCKS = 4
LRU_FLOOR = 1e-12
CHUNK_B = 128
WINDOW = 128
N_Q_C = 4
N_KV_C = 2
ROPE_THETA = 10000.0
NEG_BIG = -1e30
HGRN_HEADS = 4
PEER_HEADS = 8
PEER_KEYS = 128
PEER_DQ = 128
PEER_TOPK = 16
EPS = 1e-6
D_IN = 2560

LANES = 128
SUBLANES = 8
VMEM_LIMIT = 56 * 1024 * 1024

HG_SUB = 16
HG_SUPER = 128
HG_SAMPLE_SEQS = 4


def _cparams(sem):
    return pltpu.CompilerParams(dimension_semantics=sem, vmem_limit_bytes=VMEM_LIMIT)


_NN = (((1,), (0,)), ((), ()))
_NT = (((1,), (1,)), ((), ()))
_TN = (((0,), (0,)), ((), ()))


def _dg(a, b, dims):
    return lax.dot_general(a, b, dims, preferred_element_type=F32)


def _hi_lo(a):
    hi = a.astype(BF)
    lo = (a - hi.astype(F32)).astype(BF)
    return hi, lo


def _mm3(a, b, dims=_NN):
    ah, al = _hi_lo(a)
    bh, bl = _hi_lo(b)
    return (_dg(ah, bl, dims) + _dg(al, bh, dims)) + _dg(ah, bh, dims)


def _split3(a):
    a1 = a.astype(BF)
    r1 = a - a1.astype(F32)
    a2 = r1.astype(BF)
    a3 = (r1 - a2.astype(F32)).astype(BF)
    return a1, a2, a3


def _mm_x01(a, b01, dims=_NN):
    a1, a2, a3 = _split3(a)
    return (_dg(a3, b01, dims) + _dg(a2, b01, dims)) + _dg(a1, b01, dims)


def _mm_x01_2(a, b01, dims=_NN):
    ah, al = _hi_lo(a)
    return _dg(al, b01, dims) + _dg(ah, b01, dims)


def _mm_01x(a01, b, dims=_NN):
    b1, b2, b3 = _split3(b)
    return (_dg(a01, b3, dims) + _dg(a01, b2, dims)) + _dg(a01, b1, dims)


def _sigmoid(x):
    return jax.nn.sigmoid(x)


def _silu(x):
    return x * jax.nn.sigmoid(x)


def _gelu(x):
    return x * (0.5 * (1.0 + jnp.tanh(0.7978845608028654 * (x + 0.044715 * (x * x * x)))))


def _rms(x, w):
    return x * lax.rsqrt(jnp.mean(x * x, axis=-1, keepdims=True) + EPS) * w


def _seg_ones(width, seg):
    r = lax.broadcasted_iota(jnp.int32, (width, width), 0) // seg
    c = lax.broadcasted_iota(jnp.int32, (width, width), 1) // seg
    return jnp.where(r == c, 1.0, 0.0).astype(BF)


def _head_rms(x, gain):
    ms = _mm_x01(x * x, _seg_ones(x.shape[-1], HEAD_DIM)) * (1.0 / HEAD_DIM)
    return x * lax.rsqrt(ms + EPS) * gain


def _rope(x, cos_f, sin_s):
    w = x.shape[-1]
    lane = lax.broadcasted_iota(jnp.int32, x.shape, x.ndim - 1) % HEAD_DIM
    rot = jnp.where(lane < HEAD_DIM // 2,
                    pltpu.roll(x, w - HEAD_DIM // 2, x.ndim - 1),
                    pltpu.roll(x, HEAD_DIM // 2, x.ndim - 1))
    return x * cos_f + rot * sin_s


def _lane_head_mask(width, j):
    lane = lax.broadcasted_iota(jnp.int32, (1, width), 1) // HEAD_DIM
    return jnp.where(lane == j, 1.0, 0.0)


def _ada_body(c_ref, w_ref, b_ref, o_ref):
    c = _silu(c_ref[...])
    o_ref[...] = _mm3(c, w_ref[...]) + b_ref[...]


def _ada(c_all, w_ada, b_ada):
    n = c_all.shape[0]
    cb = 1536
    return pl.pallas_call(
        _ada_body,
        grid=(DEPTH, 6 * D_MODEL // cb),
        in_specs=[
            pl.BlockSpec((n, D_MODEL), lambda l, j: (0, 0)),
            pl.BlockSpec((None, D_MODEL, cb), lambda l, j: (l, 0, j)),
            pl.BlockSpec((None, 1, cb), lambda l, j: (l, 0, j)),
        ],
        out_specs=pl.BlockSpec((None, n, cb), lambda l, j: (l, 0, j)),
        out_shape=jax.ShapeDtypeStruct((DEPTH, n, 6 * D_MODEL), F32),
        compiler_params=_cparams(("arbitrary", "arbitrary")),
        name="ada",
    )(c_all, w_ada, b_ada.reshape(DEPTH, 1, 6 * D_MODEL))


class _Rows:
    def __init__(self, nb, nt, tn, per_row_mod):
        self.nb, self.nt, self.tn, self.per_row_mod = nb, nt, tn, per_row_mod

    def spec(self, width, col=0):
        nt = self.nt
        return pl.BlockSpec((self.tn, width), lambda b, t: (b * nt + t, col))

    def mod_spec(self, j):
        if self.per_row_mod:
            nt = self.nt
            return pl.BlockSpec((self.tn, D_MODEL), lambda b, t: (b * nt + t, j))
        return pl.BlockSpec((None, 1, D_MODEL), lambda b, t: (b, 0, j))

    @property
    def grid(self):
        return (self.nb, self.nt)

    @property
    def rows(self):
        return self.nb * self.nt * self.tn


def _const_spec(shape):
    nd = len(shape)
    return pl.BlockSpec(shape, lambda b, t: (0,) * nd)


def _in_body(x_ref, nw_ref, sh_ref, sc_ref, w_ref, z_ref):
    h = _rms(x_ref[...], nw_ref[...])
    h = h * (1.0 + sc_ref[...]) + sh_ref[...]
    z_ref[...] = jnp.dot(h.astype(BF), w_ref[...], preferred_element_type=F32)


def _in_proj(rows, x, mod, nw, w_bf):
    return pl.pallas_call(
        _in_body,
        grid=rows.grid,
        in_specs=[rows.spec(D_MODEL), _const_spec((1, D_MODEL)),
                  rows.mod_spec(0), rows.mod_spec(1), _const_spec((D_MODEL, D_IN))],
        out_specs=rows.spec(D_IN),
        out_shape=jax.ShapeDtypeStruct((rows.rows, D_IN), F32),
        compiler_params=_cparams(("arbitrary", "arbitrary")),
        name="in_proj",
    )(x, nw, mod, mod, w_bf)


def _out_body(x_ref, ya_ref, yb_ref, yc_ref, yd_ref, on_ref, w_ref, g1_ref, nf_ref,
              sh_ref, sc_ref, x1_ref, h2_ref):
    ys = []
    for g, r in enumerate((ya_ref, yb_ref, yc_ref, yd_ref)):
        ys.append(_rms(r[...], on_ref[:, g * W_GROUP:(g + 1) * W_GROUP]).astype(BF))
    o = jnp.dot(jnp.concatenate(ys, axis=-1), w_ref[...], preferred_element_type=F32)
    x1 = x_ref[...] + g1_ref[...] * o
    x1_ref[...] = x1
    h2 = _rms(x1, nf_ref[...])
    h2_ref[...] = h2 * (1.0 + sc_ref[...]) + sh_ref[...]


def _out_proj(rows, x, ys, mod, on, w_bf, nf):
    sds = jax.ShapeDtypeStruct((rows.rows, D_MODEL), F32)
    return pl.pallas_call(
        _out_body,
        grid=rows.grid,
        in_specs=[rows.spec(D_MODEL)] + [rows.spec(W_GROUP)] * 4
        + [_const_spec((1, D_MODEL)), _const_spec((D_MODEL, D_MODEL)), rows.mod_spec(2),
           _const_spec((1, D_MODEL)), rows.mod_spec(3), rows.mod_spec(4)],
        out_specs=[rows.spec(D_MODEL), rows.spec(D_MODEL)],
        out_shape=[sds, sds],
        compiler_params=_cparams(("arbitrary", "arbitrary")),
        name="out_proj",
    )(x, *ys, on, w_bf, mod, nf, mod, mod)


def _lru_gates(xc, wa, wx, ba, bx, lam):
    r = _sigmoid(_mm3(xc, wa) + ba)
    ig = _sigmoid(_mm3(xc, wx) + bx)
    nl = -lam
    sp = jnp.maximum(nl, 0.0) + jnp.log1p(jnp.exp(-jnp.abs(nl)))
    log_a = (-LRU_C) * sp * r
    a = jnp.exp(log_a)
    x2 = 2.0 * log_a
    em1 = jnp.tanh(0.5 * x2) * (jnp.exp(x2) + 1.0)
    mult = jnp.sqrt(jnp.maximum(-em1, LRU_FLOOR))
    return a, mult * (ig * xc)


def _lru_prompt_body(z_ref, cw_ref, cb_ref, wa_ref, wx_ref, ba_ref, bx_ref, lam_ref,
                     y_ref, hfin_ref, tail_ref, xbuf, hcar, *, tb):
    t = pl.program_id(1)

    @pl.when(t == 0)
    def _():
        xbuf[0:SUBLANES, :] = jnp.zeros((SUBLANES, W_GROUP), F32)
        hcar[...] = jnp.zeros_like(hcar)

    xa = z_ref[:, 0:W_GROUP]
    ga = z_ref[:, W_GROUP:2 * W_GROUP]
    xbuf[SUBLANES:SUBLANES + tb, :] = xa
    xc = cb_ref[...] + cw_ref[3:4, :] * xa
    for k in range(CONV_W - 1):
        xc = xc + cw_ref[k:k + 1, :] * xbuf[pl.ds(SUBLANES - (CONV_W - 1) + k, tb), :]
    xbuf[0:SUBLANES, :] = xa[tb - SUBLANES:tb, :]
    tail_ref[...] = xa[tb - SUBLANES:tb, :]

    a, b = _lru_gates(xc, wa_ref[...], wx_ref[...], ba_ref[...], bx_ref[...], lam_ref[...])
    row = lax.broadcasted_iota(jnp.int32, (tb, W_GROUP), 0)
    d = 1
    while d < tb:
        a_s = pltpu.roll(a, d, 0)
        b_s = pltpu.roll(b, d, 0)
        m = row >= d
        b = jnp.where(m, a * b_s + b, b)
        a = jnp.where(m, a * a_s, a)
        d *= 2
    h = b + a * hcar[0:1, :]
    hl = h[tb - 1:tb, :]
    hcar[...] = jnp.broadcast_to(hl, hcar.shape)
    hfin_ref[...] = hl
    y_ref[...] = h * _gelu(ga)


def _lru_prompt(z, nb, t_len, p, tb=512):
    nt = t_len // tb
    w2 = (1, W_GROUP)
    return pl.pallas_call(
        functools.partial(_lru_prompt_body, tb=tb),
        grid=(nb, nt),
        in_specs=[pl.BlockSpec((tb, 2 * W_GROUP), lambda b, t: (b * nt + t, 0)),
                  _const_spec((CONV_W, W_GROUP)), _const_spec(w2),
                  _const_spec((W_GROUP, W_GROUP)), _const_spec((W_GROUP, W_GROUP)),
                  _const_spec(w2), _const_spec(w2), _const_spec(w2)],
        out_specs=[pl.BlockSpec((tb, W_GROUP), lambda b, t: (b * nt + t, 0)),
                   pl.BlockSpec((None, 1, W_GROUP), lambda b, t: (b, 0, 0)),
                   pl.BlockSpec((None, SUBLANES, W_GROUP), lambda b, t: (b, 0, 0))],
        out_shape=[jax.ShapeDtypeStruct((nb * t_len, W_GROUP), F32),
                   jax.ShapeDtypeStruct((nb, 1, W_GROUP), F32),
                   jax.ShapeDtypeStruct((nb, SUBLANES, W_GROUP), F32)],
        scratch_shapes=[pltpu.VMEM((tb + SUBLANES, W_GROUP), F32),
                        pltpu.VMEM((SUBLANES, W_GROUP), F32)],
        compiler_params=_cparams(("arbitrary", "arbitrary")),
        name="lru_prompt",
    )(z, p["conv_w"], p["conv_b"], p["wa_bd"], p["wx_bd"], p["lru_ba"], p["lru_bx"], p["lru_lambda"])


def _sgu_prompt_body(z_ref, gain_ref, ws_ref, bias_ref, y_ref, *, tb):
    u = _gelu(z_ref[:, 0:W_GROUP])
    v = _rms(_gelu(z_ref[:, W_GROUP:2 * W_GROUP]), gain_ref[...])
    masks = [_lane_head_mask(W_GROUP, g) for g in range(W_GROUP // HEAD_DIM)]
    ws = ws_ref[...]
    bias = bias_ref[...]
    for j in range(tb // CHUNK_B):
        vj = v[j * CHUNK_B:(j + 1) * CHUNK_B, :]
        rhs = jnp.concatenate([vj * m for m in masks], axis=0)
        mix = _mm3(ws, rhs) + bias
        y_ref[j * CHUNK_B:(j + 1) * CHUNK_B, :] = u[j * CHUNK_B:(j + 1) * CHUNK_B, :] * mix


def _sgu_prompt(z, nb, t_len, p, tb=512):
    nt = t_len // tb
    nh = W_GROUP // HEAD_DIM
    return pl.pallas_call(
        functools.partial(_sgu_prompt_body, tb=tb),
        grid=(nb, nt),
        in_specs=[pl.BlockSpec((tb, 2 * W_GROUP), lambda b, t: (b * nt + t, 1)),
                  _const_spec((1, W_GROUP)), _const_spec((CHUNK_B, nh * CHUNK_B)),
                  _const_spec((CHUNK_B, W_GROUP))],
        out_specs=pl.BlockSpec((tb, W_GROUP), lambda b, t: (b * nt + t, 0)),
        out_shape=jax.ShapeDtypeStruct((nb * t_len, W_GROUP), F32),
        compiler_params=_cparams(("arbitrary", "arbitrary")),
        name="sgu_prompt",
    )(z, p["sgu_norm"], p["sgu_ws_cat"], p["sgu_bias"])


def _ab_sample_body(z_ref, h0_ref, buf_ref, cw_ref, cb_ref, wa_ref, wx_ref, ba_ref, bx_ref,
                    lam_ref, gain_ref, mw_ref, mb_ref,
                    ya_ref, hfin_ref, nbuf_ref, yb_ref, vrow_ref, *, nb, nt):
    xa = z_ref[:, 0:W_GROUP]
    ga = z_ref[:, W_GROUP:2 * W_GROUP]
    slabs = [buf_ref[k] for k in range(CONV_W - 1)] + [xa[t * nb:(t + 1) * nb, :] for t in range(nt)]
    xcs = []
    for t in range(nt):
        xc = cb_ref[...]
        for k in range(CONV_W):
            xc = xc + cw_ref[k:k + 1, :] * slabs[t + k]
        xcs.append(xc)
    for k in range(CONV_W - 1):
        nbuf_ref[k] = slabs[nt + k]
    xc = jnp.concatenate(xcs, axis=0)
    a, b = _lru_gates(xc, wa_ref[...], wx_ref[...], ba_ref[...], bx_ref[...], lam_ref[...])
    h = h0_ref[...]
    hs = []
    for t in range(nt):
        h = a[t * nb:(t + 1) * nb, :] * h + b[t * nb:(t + 1) * nb, :]
        hs.append(h)
    hfin_ref[...] = h
    ya_ref[...] = jnp.concatenate(hs, axis=0) * _gelu(ga)

    u = _gelu(z_ref[:, 2 * W_GROUP:3 * W_GROUP])
    v = _rms(_gelu(z_ref[:, 3 * W_GROUP:4 * W_GROUP]), gain_ref[...])
    vrow_ref[...] = v
    for t in range(nt):
        mix = mb_ref[t]
        for s in range(t + 1):
            mix = mix + mw_ref[t, s] * v[s * nb:(s + 1) * nb, :]
        yb_ref[t * nb:(t + 1) * nb, :] = u[t * nb:(t + 1) * nb, :] * mix


def _ab_sample(z, h0, buf, p, nb, nt):
    n = nb * nt
    full = lambda shape: pl.BlockSpec(shape, lambda i: (0,) * len(shape))
    w2 = (1, W_GROUP)
    sd = lambda *s: jax.ShapeDtypeStruct(s, F32)
    return pl.pallas_call(
        functools.partial(_ab_sample_body, nb=nb, nt=nt),
        grid=(1,),
        in_specs=[full((n, 4 * W_GROUP)), full((nb, W_GROUP)), full((CONV_W - 1, nb, W_GROUP)),
                  full((CONV_W, W_GROUP)), full(w2), full((W_GROUP, W_GROUP)),
                  full((W_GROUP, W_GROUP)), full(w2), full(w2), full(w2), full(w2),
                  full((nt, nt, 1, W_GROUP)), full((nt, 1, W_GROUP))],
        out_specs=[full((n, W_GROUP)), full((nb, W_GROUP)), full((CONV_W - 1, nb, W_GROUP)),
                   full((n, W_GROUP)), full((n, W_GROUP))],
        out_shape=[sd(n, W_GROUP), sd(nb, W_GROUP), sd(CONV_W - 1, nb, W_GROUP),
                   sd(n, W_GROUP), sd(n, W_GROUP)],
        compiler_params=_cparams(("arbitrary",)),
        name="ab_sample",
    )(z, h0, buf, p["conv_w"], p["conv_b"], p["wa_bd"], p["wx_bd"], p["lru_ba"], p["lru_bx"],
      p["lru_lambda"], p["sgu_norm"], p["sgu_mw_s"], p["sgu_mb_s"])


def _swa_prompt_body(sink_ref, z_ref, cos_ref, sin_ref, qg_ref, kg_ref,
                     y_ref, klast_ref, vlast_ref, kprev, vprev, *, tb):
    t = pl.program_id(1)

    @pl.when(t == 0)
    def _():
        kprev[...] = jnp.zeros_like(kprev)
        vprev[...] = jnp.zeros_like(vprev)

    cos = cos_ref[...]
    sin = sin_ref[...]
    q = _head_rms(z_ref[:, 0:2 * LANES], qg_ref[...])
    q = _rope(q, jnp.concatenate([cos, cos], axis=-1), jnp.concatenate([sin, sin], axis=-1))
    k = _rope(_head_rms(z_ref[:, 2 * LANES:3 * LANES], kg_ref[...]), cos, sin)
    v = z_ref[:, 3 * LANES:4 * LANES]
    masks = [_lane_head_mask(LANES, j) for j in range(N_KV_C)]
    r = lax.broadcasted_iota(jnp.int32, (WINDOW, 2 * WINDOW), 0)
    c = lax.broadcasted_iota(jnp.int32, (WINDOW, 2 * WINDOW), 1)
    diff = r + WINDOW - c
    band = (diff >= 0) & (diff < WINDOW)
    nsub = tb // WINDOW
    for i in range(nsub):
        sl = slice(i * WINDOW, (i + 1) * WINDOW)
        ki, vi = k[sl, :], v[sl, :]
        kk = jnp.concatenate([kprev[...], ki], axis=0)
        vv = jnp.concatenate([vprev[...], vi], axis=0)
        first = (t * nsub + i) == 0
        ok = band & ((c >= WINDOW) | jnp.logical_not(first))
        outs = []
        for g in range(N_Q_C // N_KV_C):
            qg = q[sl, g * LANES:(g + 1) * LANES]
            og = jnp.zeros((WINDOW, LANES), F32)
            for j in range(N_KV_C):
                s = _mm3(qg * masks[j], kk, _NT) * (HEAD_DIM ** -0.5)
                s = jnp.where(ok, s, NEG_BIG)
                sink = sink_ref[j * (N_Q_C // N_KV_C) + g]
                m = jnp.maximum(jnp.max(s, axis=-1, keepdims=True), sink)
                e = jnp.exp(s - m)
                den = jnp.sum(e, axis=-1, keepdims=True) + jnp.exp(sink - m)
                og = og + _mm3(e / den, vv * masks[j])
            outs.append(og)
        y_ref[sl, :] = jnp.concatenate(outs, axis=-1)
        kprev[...] = ki
        vprev[...] = vi
    klast_ref[...] = k[tb - WINDOW:tb, :]
    vlast_ref[...] = v[tb - WINDOW:tb, :]


def _swa_prompt(z, nb, t_len, p, cos_t, sin_t, tb=512):
    nt = t_len // tb
    sd = lambda *s: jax.ShapeDtypeStruct(s, F32)
    return pl.pallas_call(
        functools.partial(_swa_prompt_body, tb=tb),
        grid=(nb, nt),
        in_specs=[pl.BlockSpec(memory_space=pltpu.SMEM),
                  pl.BlockSpec((tb, 4 * LANES), lambda b, t: (b * nt + t, 2)),
                  pl.BlockSpec((tb, LANES), lambda b, t: (t, 0)),
                  pl.BlockSpec((tb, LANES), lambda b, t: (t, 0)),
                  _const_spec((1, 2 * LANES)), _const_spec((1, LANES))],
        out_specs=[pl.BlockSpec((tb, 2 * LANES), lambda b, t: (b * nt + t, 0)),
                   pl.BlockSpec((None, WINDOW, LANES), lambda b, t: (b, 0, 0)),
                   pl.BlockSpec((None, WINDOW, LANES), lambda b, t: (b, 0, 0))],
        out_shape=[sd(nb * t_len, 2 * LANES), sd(nb, WINDOW, LANES), sd(nb, WINDOW, LANES)],
        scratch_shapes=[pltpu.VMEM((WINDOW, LANES), F32), pltpu.VMEM((WINDOW, LANES), F32)],
        compiler_params=_cparams(("arbitrary", "arbitrary")),
        name="swa_prompt",
    )(p["sinks"], z, cos_t, sin_t, p["q_gain"], p["k_gain"])


def _swa_sample_body(z_ref, cos_ref, sin_ref, qg_ref, kg_ref, sink_ref, kb_ref, vb_ref,
                     y_ref, knew_ref, *, bb, nt):
    seg = _seg_ones(LANES, HEAD_DIM)
    kb = kb_ref[...]
    vb = vb_ref[...]
    nkeys = kb.shape[0]
    kidx = lax.broadcasted_iota(jnp.int32, kb.shape, 0)
    qs, ks, vs = [], [], []
    for t in range(nt):
        zt = z_ref[t]
        cos, sin = cos_ref[t], sin_ref[t]
        q = _head_rms(zt[:, 0:2 * LANES], qg_ref[...])
        qs.append(_rope(q, jnp.concatenate([cos, cos], axis=-1), jnp.concatenate([sin, sin], axis=-1)))
        kt = _rope(_head_rms(zt[:, 2 * LANES:3 * LANES], kg_ref[...]), cos, sin)
        ks.append(kt)
        vs.append(zt[:, 3 * LANES:4 * LANES])
        knew_ref[t] = kt
    scale = HEAD_DIM ** -0.5
    for t in range(nt):
        outs = []
        for g in range(N_Q_C // N_KV_C):
            qtg = qs[t][:, g * LANES:(g + 1) * LANES]
            prod = (qtg[None, :, :] * kb).reshape(nkeys * bb, LANES)
            sb = (_mm_x01(prod, seg) * scale).reshape(nkeys, bb, LANES)
            sb = jnp.where(kidx >= t + 1 + (nkeys - WINDOW), sb, NEG_BIG)
            sn = [_mm_x01(qtg * ks[s], seg) * scale for s in range(t + 1)]
            sink = sink_ref[g]
            m = jnp.maximum(jnp.max(sb, axis=0), sink)
            for x in sn:
                m = jnp.maximum(m, x)
            eb = jnp.exp(sb - m[None, :, :])
            den = jnp.sum(eb, axis=0) + jnp.exp(sink - m)
            num = jnp.sum(eb * vb, axis=0)
            for s, x in enumerate(sn):
                en = jnp.exp(x - m)
                den = den + en
                num = num + en * vs[s]
            outs.append(num / den)
        y_ref[t] = jnp.concatenate(outs, axis=-1)


def _swa_sample(z3, cos_s, sin_s, p, kb_t, vb_t, nb, nt, bb=16):
    sd = lambda *s: jax.ShapeDtypeStruct(s, F32)
    nk = kb_t.shape[0]
    c3 = lambda shape: pl.BlockSpec(shape, lambda i: (0,) * len(shape))
    return pl.pallas_call(
        functools.partial(_swa_sample_body, bb=bb, nt=nt),
        grid=(nb // bb,),
        in_specs=[pl.BlockSpec((nt, bb, 4 * LANES), lambda i: (0, i, 2)),
                  c3((nt, 1, LANES)), c3((nt, 1, LANES)), c3((1, 2 * LANES)), c3((1, LANES)),
                  c3((N_Q_C // N_KV_C, 1, LANES)),
                  pl.BlockSpec((nk, bb, LANES), lambda i: (0, i, 0)),
                  pl.BlockSpec((nk, bb, LANES), lambda i: (0, i, 0))],
        out_specs=[pl.BlockSpec((nt, bb, 2 * LANES), lambda i: (0, i, 0)),
                   pl.BlockSpec((nt, bb, LANES), lambda i: (0, i, 0))],
        out_shape=[sd(nt, nb, 2 * LANES), sd(nt, nb, LANES)],
        compiler_params=_cparams(("arbitrary",)),
        name="swa_sample",
    )(z3, cos_s, sin_s, p["q_gain"], p["k_gain"], p["sink_lanes"], kb_t, vb_t)


def _hgrn_body(*refs, rows, nq, has_init, n_valid):
    if has_init:
        (zqf_ref, zig_ref, lb_ref, gn_ref, s0_ref, y_ref, sfin_ref,
         st, qq_s, qe_s, kk_s, kd_s, bc_s, bl_s, v_s, o_s) = refs
    else:
        (zqf_ref, zig_ref, lb_ref, gn_ref, y_ref, sfin_ref,
         st, qq_s, qe_s, kk_s, kd_s, bc_s, bl_s, v_s, o_s) = refs
        s0_ref = None
    t = pl.program_id(1)

    @pl.when(t == 0)
    def _():
        if has_init:
            st[...] = s0_ref[...]
        else:
            st[...] = jnp.zeros_like(st)

    sup = HG_SUPER
    n_super = max(rows // sup, 1)
    real = min(rows, sup)
    lb = lb_ref[...]
    seg_heads = _seg_ones(W_GROUP, HEAD_DIM)
    bd_mask = seg_heads.astype(F32)
    ri = lax.broadcasted_iota(jnp.int32, (sup, sup), 0)
    ci = lax.broadcasted_iota(jnp.int32, (sup, sup), 1)
    same = (ri // HG_SUB) == (ci // HG_SUB)
    tri_bd = jnp.where(same & (ci <= ri), 1.0, 0.0).astype(BF)
    ones_bd = jnp.where(same, 1.0, 0.0).astype(BF)
    pr = lax.broadcasted_iota(jnp.int32, (HG_SUB * HG_SUB, W_GROUP), 0)
    causal = jnp.where((pr % HG_SUB) >= (pr // HG_SUB), 1.0, 0.0)

    def pad(x):
        if real == sup:
            return x
        return jnp.concatenate([x, jnp.zeros((sup - real, x.shape[1]), F32)], axis=0)

    for sb in range(n_super):
        rs = slice(sb * sup, sb * sup + real)
        for q in range(nq):
            zf = zqf_ref[q, rs, W_GROUP:2 * W_GROUP]
            f = lb + (1.0 - lb) * _sigmoid(zf)
            logf = jnp.log(jnp.maximum(f, 1e-30))
            kk = (1.0 - lb) * _sigmoid(-zf)
            if n_valid < real:
                valid = lax.broadcasted_iota(jnp.int32, (real, W_GROUP), 0) < n_valid
                logf = jnp.where(valid, logf, 0.0)
                kk = jnp.where(valid, kk, 0.0)
            logf, kk = pad(logf), pad(kk)
            qq = pad(_silu(zqf_ref[q, rs, 0:W_GROUP]))
            v = pad(zig_ref[q, rs, 0:W_GROUP])
            bc = _mm_01x(tri_bd, logf)
            bl = _mm_01x(ones_bd, logf)
            qq_s[q] = qq
            kk_s[q] = kk
            bc_s[q] = bc
            bl_s[q] = bl
            qe_s[q] = qq * jnp.exp(bc)
            kd_s[q] = kk * jnp.exp(bl - bc)
            v_s[q] = v

        n_sub = -(-real // HG_SUB)

        def sub(c, carry):
            r0 = pl.multiple_of(c * HG_SUB, HG_SUB)
            rsub = pl.ds(r0, HG_SUB)
            pms = []
            for q in range(nq):
                bc_c = bc_s[q, rsub, :]
                qq_c = qq_s[q, rsub, :]
                kk_c = kk_s[q, rsub, :]
                parts = []
                for s in range(HG_SUB):
                    e = jnp.exp(jnp.minimum(bc_c - bc_c[s:s + 1, :], 0.0))
                    parts.append(qq_c * e * kk_c[s:s + 1, :])
                pms.append(jnp.concatenate(parts, axis=0) * causal)
            atts = [_mm_x01_2(pm, seg_heads) for pm in pms]
            inter = [_mm3(qe_s[q, rsub, :], st[q], _NT) for q in range(nq)]
            upd = [_mm3(v_s[q, rsub, :], kd_s[q, rsub, :], _TN) for q in range(nq)]
            for q in range(nq):
                v_c = v_s[q, rsub, :]
                o_c = inter[q]
                for s in range(HG_SUB):
                    o_c = o_c + atts[q][s * HG_SUB:(s + 1) * HG_SUB, :] * v_c[s:s + 1, :]
                o_s[q, pl.ds(pl.multiple_of(sb * sup + r0, HG_SUB), HG_SUB), :] = o_c
                decay = jnp.exp(bl_s[q, pl.ds(r0, 1), :])
                st[q] = st[q] * decay + upd[q] * bd_mask
            return carry

        lax.fori_loop(0, n_sub, sub, 0)

    for q in range(nq):
        o = o_s[q, 0:rows, :]
        zg = zig_ref[q, :, W_GROUP:2 * W_GROUP]
        y_ref[q] = _head_rms(o, gn_ref[...]) * _silu(zg)
    sfin_ref[...] = st[...]


def _hgrn(z3, qf_col, ig_col, nq, rows, p, s0=None, n_valid=None):
    nseq, t_len, _ = z3.shape
    nt = t_len // rows
    n_valid = rows if n_valid is None else n_valid
    sd = lambda *s: jax.ShapeDtypeStruct(s, F32)
    in_specs = [pl.BlockSpec((nq, rows, 2 * W_GROUP), lambda b, t: (b, t, qf_col)),
                pl.BlockSpec((nq, rows, 2 * W_GROUP), lambda b, t: (b, t, ig_col)),
                _const_spec((1, W_GROUP)), _const_spec((1, W_GROUP))]
    args = [z3, z3, p["hgrn_lb"], p["hgrn_gn"]]
    if s0 is not None:
        in_specs.append(pl.BlockSpec((nq, W_GROUP, W_GROUP), lambda b, t: (b, 0, 0)))
        args.append(s0)
    scr_rows = max(rows, HG_SUPER)
    sup_shape = pltpu.VMEM((nq, HG_SUPER, W_GROUP), F32)
    return pl.pallas_call(
        functools.partial(_hgrn_body, rows=rows, nq=nq, has_init=s0 is not None, n_valid=n_valid),
        grid=(nseq // nq, nt),
        in_specs=in_specs,
        out_specs=[pl.BlockSpec((nq, rows, W_GROUP), lambda b, t: (b, t, 0)),
                   pl.BlockSpec((nq, W_GROUP, W_GROUP), lambda b, t: (b, 0, 0))],
        out_shape=[sd(nseq, t_len, W_GROUP), sd(nseq, W_GROUP, W_GROUP)],
        scratch_shapes=[pltpu.VMEM((nq, W_GROUP, W_GROUP), F32)] + [sup_shape] * 7
        + [pltpu.VMEM((nq, scr_rows, W_GROUP), F32)],
        compiler_params=_cparams(("arbitrary", "arbitrary")),
        name="hgrn",
    )(*args)


def _oddeven_merge_sort(n):
    pairs = []

    def merge(lo, m, r):
        step = r * 2
        if step < m:
            merge(lo, m, step)
            merge(lo + r, m, step)
            for i in range(lo + r, lo + m - r, step):
                pairs.append((i, i + r))
        else:
            pairs.append((lo, lo + r))

    def sort(lo, m):
        if m > 1:
            h = m // 2
            sort(lo, h)
            sort(lo + h, h)
            merge(lo, m, 1)

    sort(0, n)
    return pairs


_NET16 = _oddeven_merge_sort(PEER_TOPK)
_CELLS = [(a, b) for a in range(PEER_TOPK) for b in range(PEER_TOPK)
          if (a + 1) * (b + 1) <= PEER_TOPK]


def _beats(va, ia, vb, ib):
    return (va > vb) | ((va == vb) & (ia < ib))


def _cmpx(items, i, j):
    (va, ia), (vb, ib) = items[i], items[j]
    gt = _beats(va, ia, vb, ib)
    items[i] = (jnp.where(gt, va, vb), jnp.where(gt, ia, ib))
    items[j] = (jnp.where(gt, vb, va), jnp.where(gt, ib, ia))


def _merge_top(a_items, b_items):
    n = len(a_items)
    items = []
    for i in range(n):
        (va, ia), (vb, ib) = a_items[i], b_items[n - 1 - i]
        gt = _beats(va, ia, vb, ib)
        items.append((jnp.where(gt, va, vb), jnp.where(gt, ia, ib)))
    d = n // 2
    while d >= 1:
        for i in range(n):
            if not (i & d):
                _cmpx(items, i, i + d)
        d //= 2
    return items


def _top16_sorted(load_key, n_keys, shape):
    stack = []
    for g in range(n_keys // PEER_TOPK):
        items = [(load_key(g * PEER_TOPK + j), jnp.full(shape, float(g * PEER_TOPK + j), F32))
                 for j in range(PEER_TOPK)]
        for i, j in _NET16:
            _cmpx(items, i, j)
        level = 0
        while stack and stack[-1][0] == level:
            _, other = stack.pop()
            items = _merge_top(other, items)
            level += 1
        stack.append((level, items))
    assert len(stack) == 1
    return stack[0][1]


def _route_body(h_ref, wqh_ref, wql_ref, kbh_ref, kbl_ref, e1_ref, e2_ref, g_ref,
                st_s, top_s, *, tn):
    hh, hl = _hi_lo(h_ref[...])
    wqh, wql = wqh_ref[...], wql_ref[...]
    qt = (_dg(wqh, hl, _NT) + _dg(wql, hh, _NT)) + _dg(wqh, hh, _NT)
    half = PEER_HEADS * PEER_DQ // 2
    for p in range(2):
        qp = qt[p * half:(p + 1) * half, :]
        qh, ql = _hi_lo(qp)
        kh, kl = kbh_ref[p], kbl_ref[p]
        st_s[p] = (_dg(kh, ql, _NN) + _dg(kl, qh, _NN)) + _dg(kh, qh, _NN)

    shape = (PEER_HEADS, LANES)

    def column(col, carry):
        c0 = pl.multiple_of(col * LANES, LANES)
        cs = pl.ds(c0, LANES)
        for p in range(2):
            top = _top16_sorted(lambda k: st_s[p, k * PEER_HEADS:(k + 1) * PEER_HEADS, cs],
                                PEER_KEYS, shape)
            for a, (va, ia) in enumerate(top):
                top_s[p, 0, a] = va
                top_s[p, 1, a] = ia
        v1 = [top_s[0, 0, a] for a in range(PEER_TOPK)]
        v2 = [top_s[1, 0, b] for b in range(PEER_TOPK)]
        cval = {cell: v1[cell[0]] + v2[cell[1]] for cell in _CELLS}
        rank_static = {cell: float((cell[0] + 1) * (cell[1] + 1) - 1) for cell in _CELLS}
        dyn = {cell: None for cell in _CELLS}
        for xi, x in enumerate(_CELLS):
            for y in _CELLS[xi + 1:]:
                if x[0] < y[0] and x[1] > y[1]:
                    xw = jnp.where(cval[x] >= cval[y], 1.0, 0.0)
                    dyn[y] = xw if dyn[y] is None else dyn[y] + xw
                    lose = 1.0 - xw
                    dyn[x] = lose if dyn[x] is None else dyn[x] + lose
        rank = {cell: (rank_static[cell] if dyn[cell] is None else dyn[cell] + rank_static[cell])
                for cell in _CELLS}
        c00 = cval[(0, 0)]
        ex = {}
        zsum = None
        for cell in _CELLS:
            if dyn[cell] is None:
                w = jnp.exp(cval[cell] - c00)
            else:
                w = jnp.where(rank[cell] < PEER_TOPK, jnp.exp(cval[cell] - c00), 0.0)
            ex[cell] = w
            zsum = w if zsum is None else zsum + w
        inv = 1.0 / zsum
        for k in range(PEER_TOPK):
            e1 = jnp.zeros(shape, F32)
            e2 = jnp.zeros(shape, F32)
            gk = jnp.zeros(shape, F32)
            for cell in _CELLS:
                lo = (cell[0] + 1) * (cell[1] + 1) - 1
                if lo > k:
                    continue
                if dyn[cell] is None:
                    if lo != k:
                        continue
                    e1, e2, gk = top_s[0, 1, cell[0]], top_s[1, 1, cell[1]], ex[cell]
                    continue
                hit = rank[cell] == float(k)
                e1 = jnp.where(hit, top_s[0, 1, cell[0]], e1)
                e2 = jnp.where(hit, top_s[1, 1, cell[1]], e2)
                gk = jnp.where(hit, ex[cell], gk)
            rows = slice(k * PEER_HEADS, (k + 1) * PEER_HEADS)
            e1_ref[rows, cs] = e1
            e2_ref[rows, cs] = e2
            g_ref[rows, cs] = gk * inv
        return carry

    lax.fori_loop(0, tn // LANES, column, 0)


def _route(h2, p, tn=512):
    n = h2.shape[0]
    nslots = PEER_TOPK * PEER_HEADS
    half = PEER_HEADS * PEER_DQ // 2
    nk = PEER_KEYS * PEER_HEADS
    sd = jax.ShapeDtypeStruct((nslots, n), F32)
    c1 = lambda shape: pl.BlockSpec(shape, lambda i: (0,) * len(shape))
    return pl.pallas_call(
        functools.partial(_route_body, tn=tn),
        grid=(n // tn,),
        in_specs=[pl.BlockSpec((tn, D_MODEL), lambda i: (i, 0)),
                  c1((2 * half, D_MODEL)), c1((2 * half, D_MODEL)),
                  c1((2, nk, half)), c1((2, nk, half))],
        out_specs=[pl.BlockSpec((nslots, tn), lambda i: (0, i))] * 3,
        out_shape=[sd, sd, sd],
        scratch_shapes=[pltpu.VMEM((2, nk, tn), F32),
                        pltpu.VMEM((2, 2, PEER_TOPK, PEER_HEADS, LANES), F32)],
        compiler_params=_cparams(("arbitrary",)),
        name="peer_route",
    )(h2, p["wq_t_hi"], p["wq_t_lo"], p["kbig_hi"], p["kbig_lo"])


PEER_E1_STEP = 8
PEER_STEP = PEER_E1_STEP * PEER_KEYS
PEER_BUILD_UNROLL = 8 * SUBLANES
PEER_E1_HALF = PEER_KEYS // 2
PEER_STEPS_PER_HALF = PEER_E1_HALF // PEER_E1_STEP


def _gate_pitch(tn):
    assert (tn // SUBLANES) % 2 == 0
    return tn + SUBLANES


def _peer_body(h_ref, e1_ref, e2_ref, g_ref, u_ref, v_ref, x1_ref, g2_ref, o_ref,
               gt_s, xb_s, acc_s, *, tn):
    cp = pl.program_id(2)
    pitch = _gate_pitch(tn)
    step_in_half = cp % PEER_STEPS_PER_HALF

    @pl.when(cp == 0)
    def _():
        xb_s[...] = h_ref[...].astype(BF)
        acc_s[...] = jnp.zeros_like(acc_s)

    @pl.when(step_in_half == 0)
    def _():
        base = (cp // PEER_STEPS_PER_HALF * PEER_E1_HALF).astype(F32)
        sub1 = lax.broadcasted_iota(jnp.int32, (PEER_E1_HALF, LANES), 0).astype(F32) + base
        sub2 = lax.broadcasted_iota(jnp.int32, (PEER_KEYS, LANES), 0).astype(F32)

        def build(nb, carry):
            r0 = pl.multiple_of(nb * PEER_BUILD_UNROLL, PEER_BUILD_UNROLL)
            e1s = e1_ref[pl.ds(r0, PEER_BUILD_UNROLL), :]
            e2s = e2_ref[pl.ds(r0, PEER_BUILD_UNROLL), :]
            gs = g_ref[pl.ds(r0, PEER_BUILD_UNROLL), :]
            for j in range(PEER_BUILD_UNROLL):
                pt = jnp.where(sub1 == e1s[j:j + 1, :], 1.0, 0.0).astype(BF)
                qg = jnp.where(sub2 == e2s[j:j + 1, :], gs[j:j + 1, :], 0.0).astype(BF)
                gt_s[pl.ds(r0 + j, PEER_E1_HALF, stride=pitch), :] = _dg(pt, qg, _NT)
            return carry

        lax.fori_loop(0, tn // PEER_BUILD_UNROLL, build, 0)

    a = _gelu(_dg(xb_s[...], u_ref[...], _NT))
    gates = []
    for j in range(PEER_E1_STEP):
        first = pl.multiple_of((PEER_E1_STEP * step_in_half + j) * pitch, SUBLANES)
        gates.append(gt_s[pl.ds(first, tn), :])
    ga = (a * jnp.concatenate(gates, axis=-1)).astype(BF)
    acc_s[...] += jnp.dot(ga, v_ref[...], preferred_element_type=F32)

    @pl.when(cp == pl.num_programs(2) - 1)
    def _():
        o_ref[...] = x1_ref[...] + g2_ref[...] * acc_s[...]


def _peer(rows, h2, e1, e2, g, u_bf, v_bf, x1, mod):
    tn = rows.tn
    nt = rows.nt
    ncp = PEER_KEYS * PEER_KEYS // PEER_STEP
    tok = lambda width: pl.BlockSpec((tn, width), lambda b, t, c: (b * nt + t, 0))
    if rows.per_row_mod:
        g2_spec = pl.BlockSpec((tn, D_MODEL), lambda b, t, c: (b * nt + t, 5))
    else:
        g2_spec = pl.BlockSpec((None, 1, D_MODEL), lambda b, t, c: (b, 0, 5))
    return pl.pallas_call(
        functools.partial(_peer_body, tn=tn),
        grid=(rows.nb, rows.nt, ncp),
        in_specs=[tok(D_MODEL), tok(LANES), tok(LANES), tok(LANES),
                  pl.BlockSpec((PEER_STEP, D_MODEL), lambda b, t, c: (c, 0)),
                  pl.BlockSpec((PEER_STEP, D_MODEL), lambda b, t, c: (c, 0)),
                  tok(D_MODEL), g2_spec],
        out_specs=tok(D_MODEL),
        out_shape=jax.ShapeDtypeStruct((rows.rows, D_MODEL), F32),
        scratch_shapes=[pltpu.VMEM((_gate_pitch(tn) * PEER_E1_HALF, LANES), F32),
                        pltpu.VMEM((tn, D_MODEL), BF),
                        pltpu.VMEM((tn, D_MODEL), F32)],
        compiler_params=_cparams(("arbitrary", "arbitrary", "arbitrary")),
        name="peer_experts",
    )(h2, e1, e2, g, u_bf, v_bf, x1, mod)


_Q_HEAD_ORDER = (0, 2, 1, 3)


def _q_perm():
    return np.concatenate([np.arange(h * HEAD_DIM, (h + 1) * HEAD_DIM) for h in _Q_HEAD_ORDER])


def _block_diag(w):
    nblk, bi, bo = w.shape
    eye = jnp.eye(nblk, dtype=w.dtype)
    return (w[:, :, None, :] * eye[:, None, :, None]).reshape(nblk * bi, nblk * bo)


def _rope_tables(pos):
    half = HEAD_DIM // 2
    freqs = ROPE_THETA ** (-jnp.arange(half, dtype=F32) / half)
    ang = pos.astype(F32)[:, None] * freqs[None, :]
    cos, sin = jnp.cos(ang), jnp.sin(ang)
    cos_h = jnp.concatenate([cos, cos], axis=-1)
    sin_h = jnp.concatenate([-sin, sin], axis=-1)
    reps = LANES // HEAD_DIM
    return jnp.tile(cos_h, (1, reps)), jnp.tile(sin_h, (1, reps))


def _layer_params(i, nt_s, lower_bound, w_in, conv_w, conv_b, lru_wa, lru_ba, lru_wx, lru_bx,
                  lru_lambda, sgu_norm, sgu_ws, sgu_b, q_norm, k_norm, sinks, hgrn_gnorm,
                  out_norm, w_out, norm_ffn, norm_mix, peer_wq, peer_keys, peer_u, peer_v):
    row = lambda v: v.reshape(1, -1)
    qperm = _q_perm()
    q0 = 4 * W_GROUP
    in_perm = np.arange(D_IN)
    in_perm[q0:q0 + W_GROUP] = q0 + qperm
    c0 = 2 * W_GROUP
    mix_perm = np.arange(N_GROUPS * W_GROUP)
    mix_perm[c0:c0 + W_GROUP] = c0 + qperm
    nh = W_GROUP // HEAD_DIM
    causal = jnp.tril(jnp.ones((CHUNK_B, CHUNK_B), F32))
    ws_c = sgu_ws[i] * causal
    keys_pkh = peer_keys[i].transpose(1, 2, 0, 3)
    kbig = keys_pkh[:, :, :, None, :] * jnp.eye(PEER_HEADS, dtype=F32)[None, None, :, :, None]
    kbig = kbig.reshape(2, PEER_KEYS * PEER_HEADS, PEER_HEADS * PEER_DQ // 2)
    wq_t = peer_wq[i].reshape(D_MODEL, PEER_HEADS, 2, PEER_DQ // 2)
    wq_t = wq_t.transpose(2, 1, 3, 0).reshape(PEER_HEADS * PEER_DQ, D_MODEL)
    wq_hi, wq_lo = _hi_lo(wq_t)
    kb_hi, kb_lo = _hi_lo(kbig)
    sink_q = sinks[i]
    sink_lanes = jnp.stack([
        jnp.concatenate([jnp.full((HEAD_DIM,), sink_q[j * 2 + g]) for j in range(N_KV_C)])
        for g in range(N_Q_C // N_KV_C)]).reshape(N_Q_C // N_KV_C, 1, LANES)
    return {
        "norm_mix": row(norm_mix[i]),
        "w_in": w_in[i][:, in_perm].astype(BF),
        "conv_w": conv_w[i], "conv_b": row(conv_b[i]),
        "wa_bd": _block_diag(lru_wa[i]), "wx_bd": _block_diag(lru_wx[i]),
        "lru_ba": row(lru_ba[i]), "lru_bx": row(lru_bx[i]), "lru_lambda": row(lru_lambda[i]),
        "sgu_norm": row(sgu_norm[i]),
        "sgu_ws_cat": jnp.concatenate([ws_c[g] for g in range(nh)], axis=1),
        "sgu_bias": jnp.repeat(sgu_b[i].T, HEAD_DIM, axis=1),
        "sgu_mw_s": jnp.repeat(ws_c[:, :nt_s, :nt_s].transpose(1, 2, 0), HEAD_DIM,
                               axis=-1).reshape(nt_s, nt_s, 1, W_GROUP),
        "sgu_mb_s": jnp.repeat(sgu_b[i][:, :nt_s].T, HEAD_DIM, axis=-1).reshape(nt_s, 1, W_GROUP),
        "q_gain": row(jnp.tile(q_norm[i], N_Q_C)), "k_gain": row(jnp.tile(k_norm[i], N_KV_C)),
        "sinks": sink_q, "sink_lanes": sink_lanes,
        "hgrn_lb": row(lower_bound[i]), "hgrn_gn": row(jnp.tile(hgrn_gnorm[i], HGRN_HEADS)),
        "out_norm": row(out_norm[i][mix_perm]),
        "w_out": w_out[i][mix_perm, :].astype(BF),
        "norm_ffn": row(norm_ffn[i]),
        "wq_t_hi": wq_hi, "wq_t_lo": wq_lo, "kbig_hi": kb_hi, "kbig_lo": kb_lo,
        "peer_u": peer_u[i].astype(BF), "peer_v": peer_v[i].astype(BF),
    }


def _state_to_bd_t(s):
    b = s.shape[0]
    eye = jnp.eye(HGRN_HEADS, dtype=F32)
    st = jnp.swapaxes(s, 2, 3)[:, :, :, None, :] * eye[None, :, None, :, None]
    return st.reshape(b, W_GROUP, W_GROUP)


def _bd_t_to_state(st):
    b = st.shape[0]
    s5 = st.reshape(b, HGRN_HEADS, HEAD_DIM, HGRN_HEADS, HEAD_DIM)
    diag = jnp.diagonal(s5, axis1=1, axis2=3)
    return diag.transpose(0, 3, 2, 1)


def _peer_block(rows, h2, x1, mod, p):
    e1t, e2t, gt = _route(h2, p)
    return _peer(rows, h2, e1t.T, e2t.T, gt.T, p["peer_u"], p["peer_v"], x1, mod)


def kernel(x_prompt, x_sample, state_lru_h, state_lru_conv, state_swa_k, state_swa_v, state_hgrn_S,
           c_prompt, c_sample, w_ada, b_ada, norm_mix, w_in, conv_w, conv_b, lru_wa, lru_ba, lru_wx,
           lru_bx, lru_lambda, sgu_norm, sgu_ws, sgu_b, q_norm, k_norm, sinks, hgrn_lb, hgrn_gnorm,
           out_norm, w_out, norm_ffn, peer_wq, peer_keys, peer_u, peer_v):
    bp, tp, _ = x_prompt.shape
    bs, ts, _ = x_sample.shape
    n_s = bs * ts
    win = state_swa_k.shape[2]

    lbp = jax.nn.softmax(hgrn_lb.astype(F32), axis=0)
    lower_bound = jnp.cumsum(lbp, axis=0) - lbp[0]

    c_all = jnp.concatenate([c_prompt, c_sample], axis=0)
    pad_c = (-c_all.shape[0]) % SUBLANES
    mods = _ada(jnp.pad(c_all, ((0, pad_c), (0, 0))), w_ada, b_ada)

    cos_p, sin_p = _rope_tables(jnp.arange(tp, dtype=jnp.int32))
    cos_s, sin_s = _rope_tables(PAST_LEN + jnp.arange(ts, dtype=jnp.int32))
    cos_s, sin_s = cos_s.reshape(ts, 1, LANES), sin_s.reshape(ts, 1, LANES)

    rows_p = _Rows(bp, tp // 512, 512, per_row_mod=False)
    rows_s = _Rows(1, 1, n_s, per_row_mod=True)
    peer_rows_p = _Rows(bp, tp // 512, 512, per_row_mod=False)
    peer_rows_s = _Rows(1, n_s // 512, 512, per_row_mod=True)

    xp = x_prompt.reshape(bp * tp, D_MODEL)
    xs = jnp.swapaxes(x_sample, 0, 1).reshape(n_s, D_MODEL)
    st_p, st_s = [], []
    for i in range(DEPTH):
        p = _layer_params(i, ts, lower_bound, w_in, conv_w, conv_b, lru_wa, lru_ba, lru_wx, lru_bx,
                          lru_lambda, sgu_norm, sgu_ws, sgu_b, q_norm, k_norm, sinks, hgrn_gnorm,
                          out_norm, w_out, norm_ffn, norm_mix, peer_wq, peer_keys, peer_u, peer_v)
        mod_p = mods[i, :bp].reshape(bp, 1, 6 * D_MODEL)
        mod_s = jnp.tile(mods[i, bp:bp + bs], (ts, 1))

        z = _in_proj(rows_p, xp, mod_p, p["norm_mix"], p["w_in"])
        ya, h_fin, tail = _lru_prompt(z, bp, tp, p)
        yb = _sgu_prompt(z, bp, tp, p)
        yc, k_last, v_last = _swa_prompt(z, bp, tp, p, cos_p, sin_p)
        yd, s_fin = _hgrn(z.reshape(bp, tp, D_IN), 3, 4, bp, 512, p)
        yd = yd.reshape(bp * tp, W_GROUP)
        x1, h2 = _out_proj(rows_p, xp, (ya, yb, yc, yd), mod_p, p["out_norm"], p["w_out"], p["norm_ffn"])
        xp = _peer_block(peer_rows_p, h2, x1, mod_p, p)
        st_p.append((h_fin.reshape(bp, W_GROUP), tail[:, SUBLANES - (CONV_W - 1):, :],
                     k_last.reshape(bp, WINDOW, N_KV_C, HEAD_DIM),
                     v_last.reshape(bp, WINDOW, N_KV_C, HEAD_DIM), _bd_t_to_state(s_fin)))

        zs = _in_proj(rows_s, xs, mod_s, p["norm_mix"], p["w_in"])
        ya, h_fin, nbuf, yb, vrows = _ab_sample(
            zs, state_lru_h[:, i], jnp.swapaxes(state_lru_conv[:, i], 0, 1), p, bs, ts)
        kb = state_swa_k[:, i].reshape(bs, win, LANES)
        vb = state_swa_v[:, i].reshape(bs, win, LANES)
        yc3, knew = _swa_sample(zs.reshape(ts, bs, D_IN), cos_s, sin_s, p,
                                jnp.swapaxes(kb, 0, 1), jnp.swapaxes(vb, 0, 1), bs, ts)
        vnew = zs.reshape(ts, bs, D_IN)[:, :, 4 * W_GROUP + 3 * LANES:4 * W_GROUP + 4 * LANES]
        zd = jnp.swapaxes(zs.reshape(ts, bs, D_IN)[:, :, 6 * W_GROUP:], 0, 1)
        zd = jnp.pad(zd, ((0, 0), (0, HG_SUB - ts), (0, 0)))
        yd_pad, s_fin = _hgrn(zd, 0, 1, HG_SAMPLE_SEQS, HG_SUB, p,
                              s0=_state_to_bd_t(state_hgrn_S[:, i]), n_valid=ts)
        yd = jnp.swapaxes(yd_pad[:, :ts], 0, 1).reshape(n_s, W_GROUP)
        x1, h2 = _out_proj(rows_s, xs, (ya, yb, yc3.reshape(n_s, 2 * LANES), yd), mod_s,
                           p["out_norm"], p["w_out"], p["norm_ffn"])
        xs = _peer_block(peer_rows_s, h2, x1, mod_s, p)
        k_win = jnp.concatenate([kb, jnp.swapaxes(knew, 0, 1)], axis=1)[:, -win:]
        v_win = jnp.concatenate([vb, jnp.swapaxes(vnew, 0, 1)], axis=1)[:, -win:]
        st_s.append((h_fin, jnp.swapaxes(nbuf, 0, 1),
                     k_win.reshape(bs, win, N_KV_C, HEAD_DIM), v_win.reshape(bs, win, N_KV_C, HEAD_DIM),
                     _bd_t_to_state(s_fin),
                     jnp.swapaxes(vrows.reshape(ts, bs, W_GROUP), 0, 1)))

    stack = lambda per_layer, j: jnp.stack([s[j] for s in per_layer], axis=1)
    y_p = xp.reshape(bp, tp, D_MODEL)
    y_s = jnp.swapaxes(xs.reshape(ts, bs, D_MODEL), 0, 1)
    return (y_p, y_s,
            stack(st_p, 0), stack(st_p, 1), stack(st_p, 2), stack(st_p, 3), stack(st_p, 4),
            stack(st_s, 0), stack(st_s, 1), stack(st_s, 2), stack(st_s, 3), stack(st_s, 4),
            stack(st_s, 5))
```
